```python
import math
import jax, jax.numpy as jnp
from jax import lax
import numpy as np

D_MODEL = 2048
BATCH = 4
SEQ = 2048
DEPTH = 4

HEAD_DIM = 128
H_MOBA = 4
H_MLSTM = 4
H_DIL = 4
DK_MLSTM = 128
DV_MLSTM = 256
D_MIX = H_MOBA * HEAD_DIM + H_MLSTM * DV_MLSTM + H_DIL * HEAD_DIM
D_FF = 5632
D_PLE = 256
MOBA_BLOCK = 256
MOBA_TOPK = 3
MOBA_Q_CHUNK = 64
DIL_PAIRS = ((128, 1), (512, 4), (2048, 16))
DIL_BLOCK = 128
MLSTM_CHUNK = 64
CONV_WIDTH = 4
RMS_EPS = 1e-6
NEG_INF = -1e30
N_ALIBI = H_MOBA + H_DIL
IN_SIZES = (3 * H_MOBA * HEAD_DIM,
            3 * H_DIL * HEAD_DIM,
            2 * H_MLSTM * DK_MLSTM,
            H_MLSTM * DV_MLSTM,
            H_MLSTM * DV_MLSTM,
            H_MLSTM,
            H_MLSTM)
D_IN = (3 * H_MOBA * HEAD_DIM + 3 * H_DIL * HEAD_DIM + 2 * H_MLSTM * DK_MLSTM
        + 2 * H_MLSTM * DV_MLSTM + 2 * H_MLSTM)

kernel_name = 'hymba_moba_mlstm_dilated_macaron_ple'


def rms_norm(x, g):
    xf = x.astype(jnp.float32)
    y = xf * lax.rsqrt(jnp.mean(xf * xf, axis=-1, keepdims=True) + RMS_EPS)
    return (y * g.astype(jnp.float32)).astype(x.dtype)


def swiglu(h, w_up, w_down):
    gate, up = jnp.split(h @ w_up, 2, axis=-1)
    return (jax.nn.silu(gate) * up) @ w_down


def alibi_slopes():
    return 2.0 ** (-8.0 * jnp.arange(1, N_ALIBI + 1, dtype=jnp.float32) / N_ALIBI)


def split_heads(x, n):
    b, t, _ = x.shape
    return x.reshape(b, t, n, -1).transpose(0, 2, 1, 3)


def merge_heads(x):
    b, h, t, d = x.shape
    return x.transpose(0, 2, 1, 3).reshape(b, t, h * d)


def moba_attention(q, k, v, slopes):
    B, H, T, hd = q.shape
    f32 = jnp.float32
    scale = hd ** -0.5
    nb = -(-T // MOBA_BLOCK)
    Tp = nb * MOBA_BLOCK
    pad = ((0, 0), (0, 0), (0, Tp - T), (0, 0))
    qp, kp, vp = jnp.pad(q, pad), jnp.pad(k, pad), jnp.pad(v, pad)
    kb = kp.reshape(B, H, nb, MOBA_BLOCK, hd)
    vb = vp.reshape(B, H, nb, MOBA_BLOCK, hd)
    k_mean = jnp.mean(kb.astype(f32), axis=3)
    n_cols = max(nb, MOBA_TOPK)
    nq = Tp // MOBA_Q_CHUNK
    q_chunks = jnp.moveaxis(qp.reshape(B, H, nq, MOBA_Q_CHUNK, hd), 2, 0)
    b_idx = jnp.arange(B)[:, None, None, None]
    h_idx = jnp.arange(H)[None, :, None, None]
    KM = MOBA_TOPK * MOBA_BLOCK

    def one_chunk(args):
        qi, ci = args
        start = ci * MOBA_Q_CHUNK
        cur = start // MOBA_BLOCK
        q_pos = start + jnp.arange(MOBA_Q_CHUNK)
        gate = jnp.einsum('bhqd,bhnd->bhqn', qi.astype(f32), k_mean)
        gate = jnp.where(jnp.arange(nb) < cur, gate, NEG_INF)
        gate = jnp.pad(gate, ((0, 0), (0, 0), (0, 0), (0, n_cols - nb)), constant_values=NEG_INF)
        _, sel = lax.top_k(gate, MOBA_TOPK)
        sel = jnp.minimum(sel, nb - 1)
        valid = jnp.arange(MOBA_TOPK) < cur
        k_sel = kb[b_idx, h_idx, sel]
        v_sel = vb[b_idx, h_idx, sel]
        s_sel = jnp.einsum('bhqd,bhqjkd->bhqjk', qi, k_sel, preferred_element_type=f32) * scale
        dist_sel = (q_pos[:, None, None] - (sel[..., None] * MOBA_BLOCK + jnp.arange(MOBA_BLOCK))).astype(f32)
        s_sel = jnp.where(valid[:, None], s_sel - slopes[:, None, None, None] * dist_sel, NEG_INF)
        k_own = lax.dynamic_slice_in_dim(kp, cur * MOBA_BLOCK, MOBA_BLOCK, axis=2)
        v_own = lax.dynamic_slice_in_dim(vp, cur * MOBA_BLOCK, MOBA_BLOCK, axis=2)
        s_own = jnp.einsum('bhqd,bhkd->bhqk', qi, k_own, preferred_element_type=f32) * scale
        dist_own = q_pos[:, None] - (cur * MOBA_BLOCK + jnp.arange(MOBA_BLOCK))[None, :]
        s_own = jnp.where(dist_own >= 0, s_own - slopes[:, None, None] * dist_own.astype(f32), NEG_INF)
        s = jnp.concatenate([s_sel.reshape(B, H, MOBA_Q_CHUNK, KM), s_own], axis=-1)
        prob = jax.nn.softmax(s, axis=-1).astype(v.dtype)
        p_sel = prob[..., :KM].reshape(B, H, MOBA_Q_CHUNK, MOBA_TOPK, MOBA_BLOCK)
        p_own = prob[..., KM:]
        return (jnp.einsum('bhqjk,bhqjkd->bhqd', p_sel, v_sel)
                + jnp.einsum('bhqk,bhkd->bhqd', p_own, v_own))

    out = lax.map(one_chunk, (q_chunks, jnp.arange(nq)))
    out = jnp.moveaxis(out, 0, 2).reshape(B, H, Tp, hd)
    return out[:, :, :T]


def dilated_branch(q, k, v, slopes, window, dil):
    B, H, T, hd = q.shape
    f32 = jnp.float32
    scale = hd ** -0.5
    L = T // dil
    band = window // dil
    nbl = -(-L // DIL_BLOCK)
    Lp = nbl * DIL_BLOCK

    def to_blocks(a):
        a = a.reshape(B, H, L, dil, hd).transpose(0, 1, 3, 2, 4)
        a = jnp.pad(a, ((0, 0), (0, 0), (0, 0), (0, Lp - L), (0, 0)))
        return a.reshape(B, H, dil, nbl, DIL_BLOCK, hd)

    def with_prev(a):
        prev = jnp.pad(a, ((0, 0), (0, 0), (0, 0), (1, 0), (0, 0), (0, 0)))[:, :, :, :-1]
        return jnp.concatenate([prev, a], axis=4)

    qb = to_blocks(q)
    kc, vc = with_prev(to_blocks(k)), with_prev(to_blocks(v))
    s = jnp.einsum('bhrnqd,bhrnkd->bhrnqk', qb, kc, preferred_element_type=f32) * scale
    k_loc = jnp.arange(2 * DIL_BLOCK)
    delta = (DIL_BLOCK + jnp.arange(DIL_BLOCK))[:, None] - k_loc[None, :]
    k_sub = (jnp.arange(nbl)[:, None] - 1) * DIL_BLOCK + k_loc[None, :]
    mask = (delta >= 0) & (delta <= band) & (k_sub[:, None, :] >= 0)
    s = s - slopes[:, None, None, None, None] * (delta * dil).astype(f32)
    s = jnp.where(mask, s, NEG_INF)
    lse = jax.nn.logsumexp(s, axis=-1)
    prob = jnp.exp(s - lse[..., None]).astype(v.dtype)
    o = jnp.einsum('bhrnqk,bhrnkd->bhrnqd', prob, vc)

    def from_blocks(a):
        a = a.reshape(B, H, dil, Lp, *a.shape[5:])[:, :, :, :L]
        a = jnp.swapaxes(a, 2, 3)
        return a.reshape(B, H, T, *a.shape[4:])

    return from_blocks(o), from_blocks(lse)


def dilated_attention(q, k, v, slopes):
    outs, lses = [], []
    for window, dil in DIL_PAIRS:
        o, lse = dilated_branch(q, k, v, slopes, window, dil)
        outs.append(o.astype(jnp.float32))
        lses.append(lse)
    wts = jax.nn.softmax(jnp.stack(lses, axis=0), axis=0)
    return jnp.sum(wts[..., None] * jnp.stack(outs, axis=0), axis=0).astype(q.dtype)


def mlstm(q, k, v, i_pre, f_pre):
    B, H, T, dk = q.shape
    dv = v.shape[-1]
    f32 = jnp.float32
    nc = T // MLSTM_CHUNK
    q = q.astype(f32)
    k = k.astype(f32) * dk ** -0.5
    v = v.astype(f32)
    log_f = jax.nn.log_sigmoid(f_pre.astype(f32))
    log_i = i_pre.astype(f32)
    causal = jnp.tril(jnp.ones((MLSTM_CHUNK, MLSTM_CHUNK), dtype=bool))

    def chunks(a):
        return jnp.moveaxis(a.reshape(B, H, nc, MLSTM_CHUNK, *a.shape[3:]), 2, 0)

    def step(carry, xs):
        c_st, n_st, m_st = carry
        qc, kc, vc, lf, li = xs
        b = jnp.cumsum(lf, axis=-1)
        d_log = jnp.where(causal, b[..., :, None] - b[..., None, :] + li[..., None, :], NEG_INF)
        m_inter = b + m_st[..., None]
        m_t = jnp.maximum(m_inter, jnp.max(d_log, axis=-1))
        w_inter = jnp.exp(m_inter - m_t)
        s = jnp.einsum('bhtd,bhsd->bhts', qc, kc) * jnp.exp(d_log - m_t[..., None])
        num = (w_inter[..., None] * jnp.einsum('bhtd,bhde->bhte', qc, c_st)
               + jnp.einsum('bhts,bhse->bhte', s, vc))
        den = w_inter * jnp.einsum('bhtd,bhd->bht', qc, n_st) + jnp.sum(s, axis=-1)
        h = num / jnp.maximum(jnp.abs(den), jnp.exp(-m_t))[..., None]
        b_last = b[..., -1]
        g = b_last[..., None] - b + li
        m_new = jnp.maximum(b_last + m_st, jnp.max(g, axis=-1))
        w_old = jnp.exp(b_last + m_st - m_new)
        w_k = jnp.exp(g - m_new[..., None])
        c_new = w_old[..., None, None] * c_st + jnp.einsum('bhs,bhsd,bhse->bhde', w_k, kc, vc)
        n_new = w_old[..., None] * n_st + jnp.einsum('bhs,bhsd->bhd', w_k, kc)
        return (c_new, n_new, m_new), h

    init = (jnp.zeros((B, H, dk, dv), f32), jnp.zeros((B, H, dk), f32), jnp.zeros((B, H), f32))
    _, h = lax.scan(step, init, (chunks(q), chunks(k), chunks(v), chunks(log_f), chunks(log_i)))
    return jnp.moveaxis(h, 0, 2).reshape(B, H, T, dv)


def causal_conv(u, w, b):
    c = u.shape[-1]
    y = lax.conv_general_dilated(u, w.astype(u.dtype)[:, None, :], window_strides=(1,),
                                 padding=((CONV_WIDTH - 1, 0),),
                                 dimension_numbers=('NWC', 'WIO', 'NWC'), feature_group_count=c)
    return y + b.astype(u.dtype)


def token_mix(hn, w_in, conv_w, conv_b, b_igate, b_fgate, g_head, w_out, slopes_moba, slopes_dil):
    z = hn @ w_in
    offs = np.cumsum(IN_SIZES)[:-1].tolist()
    qkv_a, qkv_c, qk_m, v_m, o_m, i_m, f_m = jnp.split(z, offs, axis=-1)
    qa, ka, va = (split_heads(t, H_MOBA) for t in jnp.split(qkv_a, 3, axis=-1))
    out_a = moba_attention(qa, ka, va, slopes_moba)
    qc, kc, vc = (split_heads(t, H_DIL) for t in jnp.split(qkv_c, 3, axis=-1))
    out_c = dilated_attention(qc, kc, vc, slopes_dil)
    qk_m = jax.nn.silu(causal_conv(qk_m, conv_w, conv_b))
    qm, km = jnp.split(qk_m, 2, axis=-1)
    i_pre = jnp.swapaxes(i_m + b_igate, 1, 2)
    f_pre = jnp.swapaxes(f_m + b_fgate, 1, 2)
    h_m = mlstm(split_heads(qm, H_MLSTM), split_heads(km, H_MLSTM), split_heads(v_m, H_MLSTM), i_pre, f_pre)
    h_m = h_m * lax.rsqrt(jnp.mean(h_m * h_m, axis=-1, keepdims=True) + RMS_EPS)
    h_m = h_m * g_head.astype(jnp.float32).reshape(H_MLSTM, 1, DV_MLSTM)
    out_m = (jax.nn.sigmoid(o_m.astype(jnp.float32)) * merge_heads(h_m)).astype(hn.dtype)
    mixed = jnp.concatenate([merge_heads(out_a), out_m, merge_heads(out_c)], axis=-1)
    return mixed @ w_out


def setup_inputs(seed: int = 0) -> dict:
    key = jax.random.key(seed)
    ks = jax.random.split(key, 24)
    f32 = jnp.float32

    def normal(k, shape, scale):
        return jax.random.normal(k, shape, f32) * scale

    def gain(k, shape):
        return 1.0 + normal(k, shape, 0.02)

    return {
        'x': normal(ks[0], (BATCH, SEQ, D_MODEL), 1.0),
        'p': normal(ks[1], (DEPTH, BATCH, SEQ, D_PLE), 1.0),
        'g_ffn1': gain(ks[2], (DEPTH, D_MODEL)),
        'w_up1': normal(ks[3], (DEPTH, D_MODEL, 2 * D_FF), D_MODEL ** -0.5),
        'w_down1': normal(ks[4], (DEPTH, D_FF, D_MODEL), D_FF ** -0.5),
        'g_mix': gain(ks[5], (DEPTH, D_MODEL)),
        'w_in': normal(ks[6], (DEPTH, D_MODEL, D_IN), D_MODEL ** -0.5),
        'conv_w': normal(ks[7], (DEPTH, CONV_WIDTH, 2 * H_MLSTM * DK_MLSTM), CONV_WIDTH ** -0.5),
        'conv_b': normal(ks[8], (DEPTH, 2 * H_MLSTM * DK_MLSTM), 0.01),
        'b_igate': normal(ks[9], (DEPTH, H_MLSTM), 0.1),
        'b_fgate': jnp.linspace(3.0, 6.0, H_MLSTM, dtype=f32)[None, :] + normal(ks[10], (DEPTH, H_MLSTM), 0.1),
        'g_head': gain(ks[11], (DEPTH, H_MLSTM * DV_MLSTM)),
        'w_out': normal(ks[12], (DEPTH, D_MIX, D_MODEL), D_MIX ** -0.5),
        'g_ffn2': gain(ks[13], (DEPTH, D_MODEL)),
        'w_up2': normal(ks[14], (DEPTH, D_MODEL, 2 * D_FF), D_MODEL ** -0.5),
        'w_down2': normal(ks[15], (DEPTH, D_FF, D_MODEL), D_FF ** -0.5),
        'g_ple': gain(ks[16], (DEPTH, D_MODEL)),
        'w_ple_gate': normal(ks[17], (DEPTH, D_MODEL, D_MODEL), D_MODEL ** -0.5),
        'w_ple_proj': normal(ks[18], (DEPTH, D_PLE, D_MODEL), D_PLE ** -0.5),
        'g_final': gain(ks[19], (D_MODEL,)),
    }


def reference(x, p, g_ffn1, w_up1, w_down1, g_mix, w_in, conv_w, conv_b, b_igate, b_fgate,
              g_head, w_out, g_ffn2, w_up2, w_down2, g_ple, w_ple_gate, w_ple_proj, g_final):
    slopes = alibi_slopes()
    slopes_moba, slopes_dil = slopes[0::2], slopes[1::2]
    h = x
    for i in range(DEPTH):
        h = h + 0.5 * swiglu(rms_norm(h, g_ffn1[i]), w_up1[i], w_down1[i])
        h = h + token_mix(rms_norm(h, g_mix[i]), w_in[i], conv_w[i], conv_b[i], b_igate[i],
                          b_fgate[i], g_head[i], w_out[i], slopes_moba, slopes_dil)
        h = h + 0.5 * swiglu(rms_norm(h, g_ffn2[i]), w_up2[i], w_down2[i])
        gate = jax.nn.sigmoid(rms_norm(h, g_ple[i]) @ w_ple_gate[i])
        h = h + gate * (p[i] @ w_ple_proj[i])
    return rms_norm(h, g_final)
```

```python
import functools

import numpy as np
import jax
import jax.numpy as jnp
from jax import lax
from jax.experimental import pallas as pl
from jax.experimental.pallas import tpu as pltpu

F32 = jnp.float32
BF16 = jnp.bfloat16

D_MODEL = 2048
DEPTH = 4
HEAD_DIM = 128
H_MOBA = 4
H_MLSTM = 4
H_DIL = 4
DK_MLSTM = 128
DV_MLSTM = 256
D_FF = 5632
D_PLE = 256
MOBA_BLOCK = 256
MOBA_TOPK = 3
DIL_PAIRS = ((128, 1), (512, 4), (2048, 16))
CONV_WIDTH = 4
RMS_EPS = 1e-6
NEG_INF = -1e30
N_ALIBI = H_MOBA + H_DIL
D_ATT = 3 * H_MOBA * HEAD_DIM
D_Z = 2 * D_ATT + 2 * H_MLSTM * DK_MLSTM + 2 * H_MLSTM * DV_MLSTM
N_GATES = 2 * H_MLSTM

LANES = 128
SUBLANES = 8
VMEM_LIMIT_BYTES = 52 * 1024 * 1024

TM = 512
TF = 512
TN_IN = 1536
ATT_BLOCK = 256
MLSTM_L = 256


def _cparams(semantics):
    return pltpu.CompilerParams(dimension_semantics=semantics, vmem_limit_bytes=VMEM_LIMIT_BYTES)


def _rms(x, g):
    ms = jnp.mean(x * x, axis=-1, keepdims=True)
    return x * lax.rsqrt(ms + RMS_EPS) * g


def _dot(a, b):
    return jnp.dot(a, b, preferred_element_type=F32)


def _dot_nt(a, b):
    return lax.dot_general(a, b, (((1,), (1,)), ((), ())), preferred_element_type=F32)


def _ffn_kernel(x_ref, g_ref, wg_ref, wu_ref, wd_ref, o_ref, xn_ref, acc_ref):
    f = pl.program_id(1)

    @pl.when(f == 0)
    def _():
        xn_ref[...] = _rms(x_ref[...], g_ref[...]).astype(BF16)
        acc_ref[...] = jnp.zeros_like(acc_ref)

    xn = xn_ref[...]
    gate = _dot(xn, wg_ref[...])
    up = _dot(xn, wu_ref[...])
    act = (gate * jax.nn.sigmoid(gate) * up).astype(BF16)
    acc_ref[...] += _dot(act, wd_ref[...])

    @pl.when(f == pl.num_programs(1) - 1)
    def _():
        o_ref[...] = x_ref[...] + 0.5 * acc_ref[...]


def _ffn(h, g, w_up, w_down, layer):
    m = h.shape[0]
    nf = D_FF // TF
    return pl.pallas_call(
        _ffn_kernel,
        grid=(m // TM, nf),
        in_specs=[
            pl.BlockSpec((TM, D_MODEL), lambda i, f: (i, 0)),
            pl.BlockSpec((None, 1, D_MODEL), lambda i, f: (layer, 0, 0)),
            pl.BlockSpec((None, D_MODEL, TF), lambda i, f: (layer, 0, f)),
            pl.BlockSpec((None, D_MODEL, TF), lambda i, f: (layer, 0, f + nf)),
            pl.BlockSpec((None, TF, D_MODEL), lambda i, f: (layer, f, 0)),
        ],
        out_specs=pl.BlockSpec((TM, D_MODEL), lambda i, f: (i, 0)),
        out_shape=jax.ShapeDtypeStruct((m, D_MODEL), F32),
        scratch_shapes=[pltpu.VMEM((TM, D_MODEL), BF16), pltpu.VMEM((TM, D_MODEL), F32)],
        compiler_params=_cparams(("parallel", "arbitrary")),
        name="ffn",
    )(h, g, w_up, w_up, w_down)


def _inproj_kernel(x_ref, g_ref, w_ref, wgate_ref, wgate_t_ref, z_ref, gates_ref, gates_t_ref, xn_ref):
    n = pl.program_id(1)

    @pl.when(n == 0)
    def _():
        xn = _rms(x_ref[...], g_ref[...]).astype(BF16)
        xn_ref[...] = xn
        gates_ref[...] = _dot(xn, wgate_ref[...])
        gates_t_ref[...] = _dot_nt(wgate_t_ref[...], xn)

    z_ref[...] = _dot(xn_ref[...], w_ref[...]).astype(z_ref.dtype)


def _inproj(h, g, w_main, w_gate, w_gate_t, layer):
    m = h.shape[0]
    return pl.pallas_call(
        _inproj_kernel,
        grid=(m // TM, D_Z // TN_IN),
        in_specs=[
            pl.BlockSpec((TM, D_MODEL), lambda i, n: (i, 0)),
            pl.BlockSpec((None, 1, D_MODEL), lambda i, n: (layer, 0, 0)),
            pl.BlockSpec((None, D_MODEL, TN_IN), lambda i, n: (layer, 0, n)),
            pl.BlockSpec((None, D_MODEL, LANES), lambda i, n: (layer, 0, 0)),
            pl.BlockSpec((None, SUBLANES, D_MODEL), lambda i, n: (layer, 0, 0)),
        ],
        out_specs=[
            pl.BlockSpec((TM, TN_IN), lambda i, n: (i, n)),
            pl.BlockSpec((TM, LANES), lambda i, n: (i, 0)),
            pl.BlockSpec((SUBLANES, TM), lambda i, n: (0, i)),
        ],
        out_shape=[
            jax.ShapeDtypeStruct((m, D_Z), BF16),
            jax.ShapeDtypeStruct((m, LANES), F32),
            jax.ShapeDtypeStruct((SUBLANES, m), F32),
        ],
        scratch_shapes=[pltpu.VMEM((TM, D_MODEL), BF16)],
        compiler_params=_cparams(("parallel", "arbitrary")),
        name="inproj",
    )(h, g, w_main, w_gate, w_gate_t)


def _stage_values(v_ref, vt_ref, nblk):
    for j in range(nblk):
        vj = v_ref[j * ATT_BLOCK:(j + 1) * ATT_BLOCK, :].astype(F32)
        vt_ref[j] = vj.T.astype(BF16)


def _local_distance():
    qi = lax.broadcasted_iota(jnp.int32, (ATT_BLOCK, ATT_BLOCK), 1)
    ki = lax.broadcasted_iota(jnp.int32, (ATT_BLOCK, ATT_BLOCK), 0)
    return (qi - ki).astype(F32)


def _moba_kernel(slope_ref, q_ref, k_ref, v_ref, o_ref, kmean_ref, vt_ref, sel_ref, *, nblk):
    c = pl.program_id(2)
    slope = slope_ref[pl.program_id(1)]
    scale = HEAD_DIM ** -0.5

    @pl.when(c == 0)
    def _():
        for j in range(nblk):
            kj = k_ref[j * ATT_BLOCK:(j + 1) * ATT_BLOCK, :].astype(F32)
            kmean_ref[j:j + 1, :] = jnp.mean(kj, axis=0, keepdims=True)
        _stage_values(v_ref, vt_ref, nblk)

    q = q_ref[...]
    km = kmean_ref[...]
    km_hi = km.astype(BF16)
    km_lo = (km - km_hi.astype(F32)).astype(BF16)
    gate = _dot_nt(km_hi, q) + _dot_nt(km_lo, q)
    row = lax.broadcasted_iota(jnp.int32, gate.shape, 0)
    past = row < c
    gate = jnp.where(past, gate, NEG_INF)
    for n in range(nblk):
        gn = gate[n:n + 1, :]
        ahead = (gate > gn) | ((gate == gn) & (row < n))
        n_ahead = jnp.sum(jnp.where(ahead & past, 1.0, 0.0), axis=0, keepdims=True)
        sel_ref[n:n + 1, :] = jnp.where(n_ahead < MOBA_TOPK, 0.0, NEG_INF)

    dist = _local_distance()
    alibi = slope * dist

    k_own = k_ref[pl.ds(pl.multiple_of(c * ATT_BLOCK, ATT_BLOCK), ATT_BLOCK), :]
    s = _dot_nt(k_own, q) * scale - alibi
    s = jnp.where(dist >= 0, s, NEG_INF)
    m0 = jnp.max(s, axis=0, keepdims=True)
    p = jnp.exp(s - m0)
    l0 = jnp.sum(p, axis=0, keepdims=True)
    acc0 = _dot(vt_ref[c], p.astype(BF16))

    def body(j, carry):
        m, l, acc = carry
        kj = k_ref[pl.ds(pl.multiple_of(j * ATT_BLOCK, ATT_BLOCK), ATT_BLOCK), :]
        off = slope * ((c - j) * ATT_BLOCK).astype(F32)
        s = _dot_nt(kj, q) * scale - alibi - off + sel_ref[pl.ds(j, 1), :]
        m_new = jnp.maximum(m, jnp.max(s, axis=0, keepdims=True))
        alpha = jnp.exp(m - m_new)
        p = jnp.exp(s - m_new)
        l_new = alpha * l + jnp.sum(p, axis=0, keepdims=True)
        acc_new = alpha * acc + _dot(vt_ref[j], p.astype(BF16))
        return m_new, l_new, acc_new

    _, l, acc = lax.fori_loop(0, c, body, (m0, l0, acc0))
    o_ref[...] = (acc / l).T.astype(o_ref.dtype)


def _dil_kernel(slope_ref, q_ref, k_ref, v_ref, cnt_ref, o_ref, vt_ref, *, nblk):
    c = pl.program_id(2)
    slope = slope_ref[pl.program_id(1)]
    scale = HEAD_DIM ** -0.5

    @pl.when(c == 0)
    def _():
        _stage_values(v_ref, vt_ref, nblk)

    q = q_ref[...]
    alibi = slope * _local_distance()

    k_own = k_ref[pl.ds(pl.multiple_of(c * ATT_BLOCK, ATT_BLOCK), ATT_BLOCK), :]
    cnt = cnt_ref[0]
    s = _dot_nt(k_own, q) * scale - alibi
    s = jnp.where(cnt > 0, s, NEG_INF)
    m0 = jnp.max(s, axis=0, keepdims=True)
    p = cnt * jnp.exp(s - m0)
    l0 = jnp.sum(p, axis=0, keepdims=True)
    acc0 = _dot(vt_ref[c], p.astype(BF16))

    def body(j, carry):
        m, l, acc = carry
        kj = k_ref[pl.ds(pl.multiple_of(j * ATT_BLOCK, ATT_BLOCK), ATT_BLOCK), :]
        cnt = cnt_ref[c - j]
        off = slope * ((c - j) * ATT_BLOCK).astype(F32)
        s = _dot_nt(kj, q) * scale - alibi - off
        s = jnp.where(cnt > 0, s, NEG_INF)
        m_new = jnp.maximum(m, jnp.max(s, axis=0, keepdims=True))
        alpha = jnp.exp(m - m_new)
        p = cnt * jnp.exp(s - m_new)
        l_new = alpha * l + jnp.sum(p, axis=0, keepdims=True)
        acc_new = alpha * acc + _dot(vt_ref[j], p.astype(BF16))
        return m_new, l_new, acc_new

    _, l, acc = lax.fori_loop(0, c, body, (m0, l0, acc0))
    o_ref[...] = (acc / l).T.astype(o_ref.dtype)


def _dilated_window_counts(seq):
    nblk = seq // ATT_BLOCK
    kl = np.arange(ATT_BLOCK)[None, :, None]
    ql = np.arange(ATT_BLOCK)[None, None, :]
    d = np.arange(nblk)[:, None, None] * ATT_BLOCK + ql - kl
    cnt = np.zeros(d.shape, np.float32)
    for window, dil in DIL_PAIRS:
        cnt += ((d >= 0) & (d <= window) & (d % dil == 0)).astype(np.float32)
    return cnt


def _attention(kernel, z3, slopes, col0, extra_inputs, extra_specs, extra_scratch, name):
    b, t, _ = z3.shape
    nblk = t // ATT_BLOCK
    nh = H_MOBA
    cb = col0 // HEAD_DIM
    return pl.pallas_call(
        functools.partial(kernel, nblk=nblk),
        grid=(b, nh, nblk),
        in_specs=[
            pl.BlockSpec(memory_space=pltpu.SMEM),
            pl.BlockSpec((None, ATT_BLOCK, HEAD_DIM), lambda i, h, c: (i, c, cb + h)),
            pl.BlockSpec((None, t, HEAD_DIM), lambda i, h, c: (i, 0, cb + nh + h)),
            pl.BlockSpec((None, t, HEAD_DIM), lambda i, h, c: (i, 0, cb + 2 * nh + h)),
            *extra_specs,
        ],
        out_specs=pl.BlockSpec((None, ATT_BLOCK, HEAD_DIM), lambda i, h, c: (i, c, h)),
        out_shape=jax.ShapeDtypeStruct((b, t, nh * HEAD_DIM), BF16),
        scratch_shapes=extra_scratch,
        compiler_params=_cparams(("parallel", "parallel", "arbitrary")),
        name=name,
    )(slopes, z3, z3, z3, *extra_inputs)


def _moba(z3, slopes):
    nblk = z3.shape[1] // ATT_BLOCK
    scratch = [
        pltpu.VMEM((nblk, HEAD_DIM), F32),
        pltpu.VMEM((nblk, HEAD_DIM, ATT_BLOCK), BF16),
        pltpu.VMEM((nblk, ATT_BLOCK), F32),
    ]
    return _attention(_moba_kernel, z3, slopes, 0, (), (), scratch, "moba")


def _dilated(z3, slopes, counts):
    nblk = z3.shape[1] // ATT_BLOCK
    scratch = [pltpu.VMEM((nblk, HEAD_DIM, ATT_BLOCK), BF16)]
    spec = pl.BlockSpec((nblk, ATT_BLOCK, ATT_BLOCK), lambda i, h, c: (0, 0, 0))
    return _attention(_dil_kernel, z3, slopes, D_ATT, (counts,), (spec,), scratch, "dilated")


def _log_sigmoid(x):
    return jnp.minimum(x, 0.0) - jnp.log(1.0 + jnp.exp(-jnp.abs(x)))


def _split3(x):
    hi = x.astype(BF16)
    r1 = x - hi.astype(F32)
    mid = r1.astype(BF16)
    lo = (r1 - mid.astype(F32)).astype(BF16)
    return hi, mid, lo


def _mlstm_kernel(qk_ref, v_ref, og_ref, gates_ref, gates_t_ref, gbias_ref, gbias_t_ref, cw_ref, cb_ref,
                  ghead_ref, tril_ref, triu_ref, o_ref, ext_ref, qk_act_ref, c_ref, n_ref, m_ref):
    L = MLSTM_L
    nqk = H_MLSTM * DK_MLSTM
    chunk = pl.program_id(1)

    @pl.when(chunk == 0)
    def _():
        c_ref[...] = jnp.zeros_like(c_ref)
        n_ref[...] = jnp.zeros_like(n_ref)
        m_ref[...] = jnp.zeros_like(m_ref)
        ext_ref[0:SUBLANES, :] = jnp.zeros((SUBLANES, 2 * nqk), F32)

    u = qk_ref[...].astype(F32)
    ext_ref[SUBLANES:SUBLANES + L, :] = u
    cw = cw_ref[...]
    y = cb_ref[...] + cw[CONV_WIDTH - 1:CONV_WIDTH, :] * u
    for back in range(1, CONV_WIDTH):
        w_row = cw[CONV_WIDTH - 1 - back:CONV_WIDTH - back, :]
        y = y + w_row * ext_ref[pl.ds(SUBLANES - back, L), :]
    ext_ref[0:SUBLANES, :] = u[L - SUBLANES:L, :]
    qk_act_ref[...] = y * jax.nn.sigmoid(y)

    g_col = gates_ref[...] + gbias_ref[...]
    g_row = gates_t_ref[...] + gbias_t_ref[...]
    lf_col = _log_sigmoid(g_col)
    lf_row = _log_sigmoid(g_row)
    b_col_all = sum(_dot(tril_ref[...], t) for t in _split3(lf_col))
    b_row_all = sum(_dot(t, triu_ref[...]) for t in _split3(lf_row))

    ti = lax.broadcasted_iota(jnp.int32, (L, L), 0)
    si = lax.broadcasted_iota(jnp.int32, (L, L), 1)
    causal = si <= ti

    for h in range(H_MLSTM):
        q = qk_act_ref[:, h * DK_MLSTM:(h + 1) * DK_MLSTM]
        k = qk_act_ref[:, nqk + h * DK_MLSTM:nqk + (h + 1) * DK_MLSTM] * (DK_MLSTM ** -0.5)
        v = v_ref[:, h * DV_MLSTM:(h + 1) * DV_MLSTM]
        qb = q.astype(BF16)
        kb = k.astype(BF16)
        li_col = g_col[:, h:h + 1]
        li_row = g_row[h:h + 1, :]
        b_col = b_col_all[:, H_MLSTM + h:H_MLSTM + h + 1]
        b_row = b_row_all[H_MLSTM + h:H_MLSTM + h + 1, :]
        c_st = c_ref[h]
        n_st = n_ref[h]
        m_st = m_ref[h]

        d_log = jnp.where(causal, b_col - b_row + li_row, NEG_INF)
        m_inter = b_col + m_st
        m_t = jnp.maximum(m_inter, jnp.max(d_log, axis=-1, keepdims=True))
        w_inter = jnp.exp(m_inter - m_t)
        s = _dot_nt(qb, kb) * jnp.exp(d_log - m_t)
        num = w_inter * _dot(qb, c_st.astype(BF16)) + _dot(s.astype(BF16), v)
        den = w_inter * jnp.sum(q * n_st, axis=-1, keepdims=True) + jnp.sum(s, axis=-1, keepdims=True)
        hh = num / jnp.maximum(jnp.abs(den), jnp.exp(-m_t))
        hh = hh * lax.rsqrt(jnp.mean(hh * hh, axis=-1, keepdims=True) + RMS_EPS)
        hh = hh * ghead_ref[:, h * DV_MLSTM:(h + 1) * DV_MLSTM]
        og = og_ref[:, h * DV_MLSTM:(h + 1) * DV_MLSTM].astype(F32)
        o_ref[:, h * DV_MLSTM:(h + 1) * DV_MLSTM] = (jax.nn.sigmoid(og) * hh).astype(o_ref.dtype)

        b_last = b_col[L - 1:L, :]
        g_dec_row = b_last - b_row + li_row
        g_dec_col = b_last - b_col + li_col
        m_new = jnp.maximum(b_last + m_st, jnp.max(g_dec_row, axis=-1, keepdims=True))
        w_old = jnp.exp(b_last + m_st - m_new)
        kw = k * jnp.exp(g_dec_col - m_new)
        c_ref[h] = w_old * c_st + _dot(kw.T.astype(BF16), v)
        n_ref[h] = w_old * n_st + jnp.sum(kw, axis=0, keepdims=True)
        m_ref[h] = m_new


def _mlstm(z3, gates3, gates_t, gbias, gbias_t, conv_w, conv_b, g_head, tril, triu, layer):
    b, t, _ = z3.shape
    L = MLSTM_L
    nc = t // L
    nqk = 2 * H_MLSTM * DK_MLSTM
    nv = H_MLSTM * DV_MLSTM
    qk_blk = (2 * D_ATT) // nqk
    v_blk = (2 * D_ATT + nqk) // nv
    og_blk = v_blk + 1
    return pl.pallas_call(
        _mlstm_kernel,
        grid=(b, nc),
        in_specs=[
            pl.BlockSpec((None, L, nqk), lambda i, c: (i, c, qk_blk)),
            pl.BlockSpec((None, L, nv), lambda i, c: (i, c, v_blk)),
            pl.BlockSpec((None, L, nv), lambda i, c: (i, c, og_blk)),
            pl.BlockSpec((None, L, LANES), lambda i, c: (i, c, 0)),
            pl.BlockSpec((SUBLANES, L), lambda i, c: (0, i * nc + c)),
            pl.BlockSpec((None, 1, LANES), lambda i, c: (layer, 0, 0)),
            pl.BlockSpec((None, SUBLANES, 1), lambda i, c: (layer, 0, 0)),
            pl.BlockSpec((None, CONV_WIDTH, nqk), lambda i, c: (layer, 0, 0)),
            pl.BlockSpec((None, 1, nqk), lambda i, c: (layer, 0, 0)),
            pl.BlockSpec((None, 1, nv), lambda i, c: (layer, 0, 0)),
            pl.BlockSpec((L, L), lambda i, c: (0, 0)),
            pl.BlockSpec((L, L), lambda i, c: (0, 0)),
        ],
        out_specs=pl.BlockSpec((None, L, nv), lambda i, c: (i, c, 0)),
        out_shape=jax.ShapeDtypeStruct((b, t, nv), BF16),
        scratch_shapes=[
            pltpu.VMEM((L + SUBLANES, nqk), F32),
            pltpu.VMEM((L, nqk), F32),
            pltpu.VMEM((H_MLSTM, DK_MLSTM, DV_MLSTM), F32),
            pltpu.VMEM((H_MLSTM, 1, DK_MLSTM), F32),
            pltpu.VMEM((H_MLSTM, 1, 1), F32),
        ],
        compiler_params=_cparams(("parallel", "arbitrary")),
        name="mlstm",
    )(z3, z3, z3, gates3, gates_t, gbias, gbias_t, conv_w, conv_b, g_head, tril, triu)


def _outproj_kernel(h_ref, a_ref, m_ref, c_ref, w_ref, o_ref):
    na = H_MOBA * HEAD_DIM
    nm = H_MLSTM * DV_MLSTM
    acc = _dot(a_ref[...], w_ref[0:na, :])
    acc += _dot(m_ref[...], w_ref[na:na + nm, :])
    acc += _dot(c_ref[...], w_ref[na + nm:, :])
    o_ref[...] = h_ref[...] + acc


def _outproj(h, out_a, out_m, out_c, w_out, layer):
    m = h.shape[0]
    na, nm, ncol = out_a.shape[1], out_m.shape[1], out_c.shape[1]
    return pl.pallas_call(
        _outproj_kernel,
        grid=(m // TM,),
        in_specs=[
            pl.BlockSpec((TM, D_MODEL), lambda i: (i, 0)),
            pl.BlockSpec((TM, na), lambda i: (i, 0)),
            pl.BlockSpec((TM, nm), lambda i: (i, 0)),
            pl.BlockSpec((TM, ncol), lambda i: (i, 0)),
            pl.BlockSpec((None, na + nm + ncol, D_MODEL), lambda i: (layer, 0, 0)),
        ],
        out_specs=pl.BlockSpec((TM, D_MODEL), lambda i: (i, 0)),
        out_shape=jax.ShapeDtypeStruct((m, D_MODEL), F32),
        compiler_params=_cparams(("parallel",)),
        name="outproj",
    )(h, out_a, out_m, out_c, w_out)


def _ple_kernel(h_ref, g_ref, p_ref, wg_ref, wp_ref, gfin_ref, o_ref, *, final):
    x = h_ref[...]
    xn = _rms(x, g_ref[...]).astype(BF16)
    gate = jax.nn.sigmoid(_dot(xn, wg_ref[...]))
    proj = _dot(p_ref[...].astype(BF16), wp_ref[...])
    y = x + gate * proj
    if final:
        y = _rms(y, gfin_ref[...])
    o_ref[...] = y


def _ple(h, g, p2, w_gate, w_proj, g_final, layer, final):
    m = h.shape[0]
    return pl.pallas_call(
        functools.partial(_ple_kernel, final=final),
        grid=(m // TM,),
        in_specs=[
            pl.BlockSpec((TM, D_MODEL), lambda i: (i, 0)),
            pl.BlockSpec((None, 1, D_MODEL), lambda i: (layer, 0, 0)),
            pl.BlockSpec((None, TM, D_PLE), lambda i: (layer, i, 0)),
            pl.BlockSpec((None, D_MODEL, D_MODEL), lambda i: (layer, 0, 0)),
            pl.BlockSpec((None, D_PLE, D_MODEL), lambda i: (layer, 0, 0)),
            pl.BlockSpec((1, D_MODEL), lambda i: (0, 0)),
        ],
        out_specs=pl.BlockSpec((TM, D_MODEL), lambda i: (i, 0)),
        out_shape=jax.ShapeDtypeStruct((m, D_MODEL), F32),
        compiler_params=_cparams(("parallel",)),
        name="ple",
    )(h, g, p2, w_gate, w_proj, g_final)


def kernel(x, p, g_ffn1, w_up1, w_down1, g_mix, w_in, conv_w, conv_b, b_igate, b_fgate, g_head, w_out, g_ffn2, w_up2, w_down2, g_ple, w_ple_gate, w_ple_proj, g_final):
    b, t, d = x.shape
    depth = p.shape[0]
    m = b * t

    w_up1b, w_down1b = w_up1.astype(BF16), w_down1.astype(BF16)
    w_up2b, w_down2b = w_up2.astype(BF16), w_down2.astype(BF16)
    w_in_main = w_in[:, :, :D_Z].astype(BF16)
    w_in_gate = w_in[:, :, D_Z:]
    w_gate_col = jnp.pad(w_in_gate, ((0, 0), (0, 0), (0, LANES - N_GATES))).astype(BF16)
    w_gate_row = jnp.swapaxes(w_in_gate, 1, 2).astype(BF16)
    w_outb = w_out.astype(BF16)
    w_pgb, w_ppb = w_ple_gate.astype(BF16), w_ple_proj.astype(BF16)
    gate_bias = jnp.concatenate([b_igate, b_fgate], axis=-1)
    gbias_col = jnp.pad(gate_bias, ((0, 0), (0, LANES - N_GATES)))[:, None, :]
    gbias_row = gate_bias[:, :, None]
    row = lambda a: a[:, None, :]
    g_ffn1r, g_mixr, g_ffn2r, g_pler = row(g_ffn1), row(g_mix), row(g_ffn2), row(g_ple)
    conv_br, g_headr = row(conv_b), row(g_head)
    g_finalr = g_final[None, :]
    p2 = p.reshape(depth, m, D_PLE)

    slopes = 2.0 ** (-8.0 * jnp.arange(1, N_ALIBI + 1, dtype=F32) / N_ALIBI)
    slopes_moba, slopes_dil = slopes[0::2], slopes[1::2]
    counts = jnp.asarray(_dilated_window_counts(t))
    tri = np.tril(np.ones((MLSTM_L, MLSTM_L), np.float32))
    tril, triu = jnp.asarray(tri, BF16), jnp.asarray(tri.T, BF16)

    h = x.reshape(m, d)
    for i in range(depth):
        h = _ffn(h, g_ffn1r, w_up1b, w_down1b, i)
        z, gates, gates_t = _inproj(h, g_mixr, w_in_main, w_gate_col, w_gate_row, i)
        z3 = z.reshape(b, t, D_Z)
        out_a = _moba(z3, slopes_moba)
        out_c = _dilated(z3, slopes_dil, counts)
        out_m = _mlstm(z3, gates.reshape(b, t, LANES), gates_t, gbias_col, gbias_row,
                       conv_w, conv_br, g_headr, tril, triu, i)
        h = _outproj(h, out_a.reshape(m, -1), out_m.reshape(m, -1), out_c.reshape(m, -1), w_outb, i)
        h = _ffn(h, g_ffn2r, w_up2b, w_down2b, i)
        h = _ple(h, g_pler, p2, w_pgb, w_ppb, g_finalr, i, final=(i == depth - 1))
    return h.reshape(b, t, d)
```

```python
import functools

import numpy as np
import jax
import jax.numpy as jnp
from jax import lax
from jax.experimental import pallas as pl
from jax.experimental.pallas import tpu as pltpu

F32 = jnp.float32
BF16 = jnp.bfloat16

D_MODEL = 2048
DEPTH = 4
HEAD_DIM = 128
H_MOBA = 4
H_MLSTM = 4
H_DIL = 4
DK_MLSTM = 128
DV_MLSTM = 256
D_FF = 5632
D_PLE = 256
MOBA_BLOCK = 256
MOBA_TOPK = 3
DIL_PAIRS = ((128, 1), (512, 4), (2048, 16))
CONV_WIDTH = 4
RMS_EPS = 1e-6
NEG_INF = -1e30
N_ALIBI = H_MOBA + H_DIL
D_ATT = 3 * H_MOBA * HEAD_DIM
D_Z = 2 * D_ATT + 2 * H_MLSTM * DK_MLSTM + 2 * H_MLSTM * DV_MLSTM
N_GATES = 2 * H_MLSTM

LANES = 128
SUBLANES = 8
VMEM_LIMIT_BYTES = 52 * 1024 * 1024

TM = 512
TF = 512
TN_IN = 1536
ATT_BLOCK = 256
MLSTM_L = 256


def _cparams(semantics):
    return pltpu.CompilerParams(dimension_semantics=semantics, vmem_limit_bytes=VMEM_LIMIT_BYTES)


def _rms(x, g):
    ms = jnp.mean(x * x, axis=-1, keepdims=True)
    return x * lax.rsqrt(ms + RMS_EPS) * g


def _dot(a, b):
    return jnp.dot(a, b, preferred_element_type=F32)


def _dot_nt(a, b):
    return lax.dot_general(a, b, (((1,), (1,)), ((), ())), preferred_element_type=F32)


def _ffn_kernel(x_ref, g_ref, wg_ref, wu_ref, wd_ref, o_ref, xn_ref, acc_ref):
    f = pl.program_id(1)

    @pl.when(f == 0)
    def _():
        xn_ref[...] = _rms(x_ref[...], g_ref[...]).astype(BF16)
        acc_ref[...] = jnp.zeros_like(acc_ref)

    xn = xn_ref[...]
    gate = _dot(xn, wg_ref[...])
    up = _dot(xn, wu_ref[...])
    act = (gate * jax.nn.sigmoid(gate) * up).astype(BF16)
    acc_ref[...] += _dot(act, wd_ref[...])

    @pl.when(f == pl.num_programs(1) - 1)
    def _():
        o_ref[...] = x_ref[...] + 0.5 * acc_ref[...]


def _ffn(h, g, w_up, w_down, layer):
    m = h.shape[0]
    nf = D_FF // TF
    return pl.pallas_call(
        _ffn_kernel,
        grid=(m // TM, nf),
        in_specs=[
            pl.BlockSpec((TM, D_MODEL), lambda i, f: (i, 0)),
            pl.BlockSpec((None, 1, D_MODEL), lambda i, f: (layer, 0, 0)),
            pl.BlockSpec((None, D_MODEL, TF), lambda i, f: (layer, 0, f)),
            pl.BlockSpec((None, D_MODEL, TF), lambda i, f: (layer, 0, f + nf)),
            pl.BlockSpec((None, TF, D_MODEL), lambda i, f: (layer, f, 0)),
        ],
        out_specs=pl.BlockSpec((TM, D_MODEL), lambda i, f: (i, 0)),
        out_shape=jax.ShapeDtypeStruct((m, D_MODEL), F32),
        scratch_shapes=[pltpu.VMEM((TM, D_MODEL), BF16), pltpu.VMEM((TM, D_MODEL), F32)],
        compiler_params=_cparams(("parallel", "arbitrary")),
        name="ffn",
    )(h, g, w_up, w_up, w_down)


def _inproj_kernel(x_ref, g_ref, w_ref, wgate_ref, wgate_t_ref, z_ref, gates_ref, gates_t_ref, xn_ref):
    n = pl.program_id(1)

    @pl.when(n == 0)
    def _():
        xn = _rms(x_ref[...], g_ref[...]).astype(BF16)
        xn_ref[...] = xn
        gates_ref[...] = _dot(xn, wgate_ref[...])
        gates_t_ref[...] = _dot_nt(wgate_t_ref[...], xn)

    z_ref[...] = _dot(xn_ref[...], w_ref[...]).astype(z_ref.dtype)


def _inproj(h, g, w_main, w_gate, w_gate_t, layer):
    m = h.shape[0]
    return pl.pallas_call(
        _inproj_kernel,
        grid=(m // TM, D_Z // TN_IN),
        in_specs=[
            pl.BlockSpec((TM, D_MODEL), lambda i, n: (i, 0)),
            pl.BlockSpec((None, 1, D_MODEL), lambda i, n: (layer, 0, 0)),
            pl.BlockSpec((None, D_MODEL, TN_IN), lambda i, n: (layer, 0, n)),
            pl.BlockSpec((None, D_MODEL, LANES), lambda i, n: (layer, 0, 0)),
            pl.BlockSpec((None, SUBLANES, D_MODEL), lambda i, n: (layer, 0, 0)),
        ],
        out_specs=[
            pl.BlockSpec((TM, TN_IN), lambda i, n: (i, n)),
            pl.BlockSpec((TM, LANES), lambda i, n: (i, 0)),
            pl.BlockSpec((SUBLANES, TM), lambda i, n: (0, i)),
        ],
        out_shape=[
            jax.ShapeDtypeStruct((m, D_Z), BF16),
            jax.ShapeDtypeStruct((m, LANES), F32),
            jax.ShapeDtypeStruct((SUBLANES, m), F32),
        ],
        scratch_shapes=[pltpu.VMEM((TM, D_MODEL), BF16)],
        compiler_params=_cparams(("parallel", "arbitrary")),
        name="inproj",
    )(h, g, w_main, w_gate, w_gate_t)


def _stage_values(v_ref, vt_ref, nblk):
    for j in range(nblk):
        vj = v_ref[j * ATT_BLOCK:(j + 1) * ATT_BLOCK, :].astype(F32)
        vt_ref[:, j * ATT_BLOCK:(j + 1) * ATT_BLOCK] = vj.T.astype(BF16)


def _local_distance():
    qi = lax.broadcasted_iota(jnp.int32, (ATT_BLOCK, ATT_BLOCK), 1)
    ki = lax.broadcasted_iota(jnp.int32, (ATT_BLOCK, ATT_BLOCK), 0)
    return (qi - ki).astype(F32)


def _fold_sublanes(x, op):
    return op(x.reshape(ATT_BLOCK // SUBLANES, SUBLANES, x.shape[-1]), axis=0)


def _softmax_pv(cc, score_fn, weight_fn, s_ref, p_ref, vt_ref, o_ref):
    nk = (cc + 1) * ATT_BLOCK
    m8 = None
    for j in range(cc + 1):
        s = score_fn(j)
        s_ref[j * ATT_BLOCK:(j + 1) * ATT_BLOCK, :] = s
        part = _fold_sublanes(s, jnp.max)
        m8 = part if m8 is None else jnp.maximum(m8, part)
    m = jnp.max(m8, axis=0, keepdims=True)
    l8 = None
    for j in range(cc + 1):
        p = jnp.exp(s_ref[j * ATT_BLOCK:(j + 1) * ATT_BLOCK, :] - m)
        w = weight_fn(j)
        if w is not None:
            p = p * w
        p_ref[j * ATT_BLOCK:(j + 1) * ATT_BLOCK, :] = p.astype(BF16)
        part = _fold_sublanes(p, jnp.sum)
        l8 = part if l8 is None else l8 + part
    l = jnp.sum(l8, axis=0, keepdims=True)
    acc = _dot(vt_ref[:, 0:nk], p_ref[0:nk, :])
    o_ref[...] = (acc / l).T.astype(o_ref.dtype)


def _moba_kernel(slope_ref, q_ref, k_ref, v_ref, o_ref, kmean_ref, vt_ref, s_ref, p_ref, *, nblk):
    c = pl.program_id(2)
    slope = slope_ref[pl.program_id(1)]
    scale = HEAD_DIM ** -0.5

    @pl.when(c == 0)
    def _():
        for j in range(nblk):
            kj = k_ref[j * ATT_BLOCK:(j + 1) * ATT_BLOCK, :].astype(F32)
            kmean_ref[j:j + 1, :] = jnp.mean(kj, axis=0, keepdims=True)
        _stage_values(v_ref, vt_ref, nblk)

    def branch(cc):
        q = q_ref[...]
        dist = _local_distance()
        alibi = slope * dist
        if cc > MOBA_TOPK:
            km = kmean_ref[...]
            km_hi = km.astype(BF16)
            km_lo = (km - km_hi.astype(F32)).astype(BF16)
            gate = _dot_nt(km_hi, q) + _dot_nt(km_lo, q)
            g = [gate[n:n + 1, :] for n in range(cc)]
            sel = []
            for n in range(cc):
                n_ahead = jnp.zeros_like(g[n])
                for mth in range(cc):
                    if mth != n:
                        ahead = (g[mth] >= g[n]) if mth < n else (g[mth] > g[n])
                        n_ahead = n_ahead + jnp.where(ahead, 1.0, 0.0)
                sel.append(jnp.where(n_ahead < MOBA_TOPK, 0.0, NEG_INF))
        else:
            sel = None

        def score(j):
            kj = k_ref[j * ATT_BLOCK:(j + 1) * ATT_BLOCK, :]
            s = _dot_nt(kj, q) * scale - alibi
            if j == cc:
                return jnp.where(dist >= 0, s, NEG_INF)
            s = s - slope * float((cc - j) * ATT_BLOCK)
            return s if sel is None else s + sel[j]

        _softmax_pv(cc, score, lambda j: None, s_ref, p_ref, vt_ref, o_ref)

    for cc in range(nblk):
        pl.when(c == cc)(functools.partial(branch, cc))


def _dil_kernel(slope_ref, q_ref, k_ref, v_ref, cnt_ref, o_ref, vt_ref, s_ref, p_ref, *, nblk):
    c = pl.program_id(2)
    slope = slope_ref[pl.program_id(1)]
    scale = HEAD_DIM ** -0.5

    @pl.when(c == 0)
    def _():
        _stage_values(v_ref, vt_ref, nblk)

    def branch(cc):
        q = q_ref[...]
        alibi = slope * _local_distance()

        def score(j):
            kj = k_ref[j * ATT_BLOCK:(j + 1) * ATT_BLOCK, :]
            s = _dot_nt(kj, q) * scale - alibi - slope * float((cc - j) * ATT_BLOCK)
            return jnp.where(cnt_ref[cc - j] > 0, s, NEG_INF)

        _softmax_pv(cc, score, lambda j: cnt_ref[cc - j], s_ref, p_ref, vt_ref, o_ref)

    for cc in range(nblk):
        pl.when(c == cc)(functools.partial(branch, cc))


def _dilated_window_counts(seq):
    nblk = seq // ATT_BLOCK
    kl = np.arange(ATT_BLOCK)[None, :, None]
    ql = np.arange(ATT_BLOCK)[None, None, :]
    d = np.arange(nblk)[:, None, None] * ATT_BLOCK + ql - kl
    cnt = np.zeros(d.shape, np.float32)
    for window, dil in DIL_PAIRS:
        cnt += ((d >= 0) & (d <= window) & (d % dil == 0)).astype(np.float32)
    return cnt


def _attention(kernel, z3, slopes, col0, extra_inputs, extra_specs, extra_scratch, name):
    b, t, _ = z3.shape
    nblk = t // ATT_BLOCK
    nh = H_MOBA
    cb = col0 // HEAD_DIM
    scratch = [
        *extra_scratch,
        pltpu.VMEM((HEAD_DIM, t), BF16),
        pltpu.VMEM((t, ATT_BLOCK), F32),
        pltpu.VMEM((t, ATT_BLOCK), BF16),
    ]
    return pl.pallas_call(
        functools.partial(kernel, nblk=nblk),
        grid=(b, nh, nblk),
        in_specs=[
            pl.BlockSpec(memory_space=pltpu.SMEM),
            pl.BlockSpec((None, ATT_BLOCK, HEAD_DIM), lambda i, h, c: (i, c, cb + h)),
            pl.BlockSpec((None, t, HEAD_DIM), lambda i, h, c: (i, 0, cb + nh + h)),
            pl.BlockSpec((None, t, HEAD_DIM), lambda i, h, c: (i, 0, cb + 2 * nh + h)),
            *extra_specs,
        ],
        out_specs=pl.BlockSpec((None, ATT_BLOCK, HEAD_DIM), lambda i, h, c: (i, c, h)),
        out_shape=jax.ShapeDtypeStruct((b, t, nh * HEAD_DIM), BF16),
        scratch_shapes=scratch,
        compiler_params=_cparams(("parallel", "parallel", "arbitrary")),
        name=name,
    )(slopes, z3, z3, z3, *extra_inputs)


def _moba(z3, slopes):
    nblk = z3.shape[1] // ATT_BLOCK
    scratch = [pltpu.VMEM((nblk, HEAD_DIM), F32)]
    return _attention(_moba_kernel, z3, slopes, 0, (), (), scratch, "moba")


def _dilated(z3, slopes, counts):
    nblk = z3.shape[1] // ATT_BLOCK
    spec = pl.BlockSpec((nblk, ATT_BLOCK, ATT_BLOCK), lambda i, h, c: (0, 0, 0))
    return _attention(_dil_kernel, z3, slopes, D_ATT, (counts,), (spec,), [], "dilated")


def _log_sigmoid(x):
    return jnp.minimum(x, 0.0) - jnp.log(1.0 + jnp.exp(-jnp.abs(x)))


def _split3(x):
    hi = x.astype(BF16)
    r1 = x - hi.astype(F32)
    mid = r1.astype(BF16)
    lo = (r1 - mid.astype(F32)).astype(BF16)
    return hi, mid, lo


def _mlstm_kernel(qk_ref, v_ref, og_ref, gates_ref, gates_t_ref, gbias_ref, gbias_t_ref, cw_ref, cb_ref,
                  ghead_ref, tril_ref, triu_ref, o_ref, ext_ref, qk_act_ref, c_ref, n_ref, m_ref):
    L = MLSTM_L
    nqk = H_MLSTM * DK_MLSTM
    chunk = pl.program_id(1)

    @pl.when(chunk == 0)
    def _():
        c_ref[...] = jnp.zeros_like(c_ref)
        n_ref[...] = jnp.zeros_like(n_ref)
        m_ref[...] = jnp.zeros_like(m_ref)
        ext_ref[0:SUBLANES, :] = jnp.zeros((SUBLANES, 2 * nqk), F32)

    u = qk_ref[...].astype(F32)
    ext_ref[SUBLANES:SUBLANES + L, :] = u
    cw = cw_ref[...]
    y = cb_ref[...] + cw[CONV_WIDTH - 1:CONV_WIDTH, :] * u
    for back in range(1, CONV_WIDTH):
        w_row = cw[CONV_WIDTH - 1 - back:CONV_WIDTH - back, :]
        y = y + w_row * ext_ref[pl.ds(SUBLANES - back, L), :]
    ext_ref[0:SUBLANES, :] = u[L - SUBLANES:L, :]
    qk_act_ref[...] = y * jax.nn.sigmoid(y)

    g_col = gates_ref[...] + gbias_ref[...]
    g_row = gates_t_ref[...] + gbias_t_ref[...]
    lf_col = _log_sigmoid(g_col)
    lf_row = _log_sigmoid(g_row)
    b_col_all = sum(_dot(tril_ref[...], t) for t in _split3(lf_col))
    b_row_all = sum(_dot(t, triu_ref[...]) for t in _split3(lf_row))

    ti = lax.broadcasted_iota(jnp.int32, (L, L), 0)
    si = lax.broadcasted_iota(jnp.int32, (L, L), 1)
    causal = si <= ti

    for h in range(H_MLSTM):
        q = qk_act_ref[:, h * DK_MLSTM:(h + 1) * DK_MLSTM]
        k = qk_act_ref[:, nqk + h * DK_MLSTM:nqk + (h + 1) * DK_MLSTM] * (DK_MLSTM ** -0.5)
        v = v_ref[:, h * DV_MLSTM:(h + 1) * DV_MLSTM]
        qb = q.astype(BF16)
        kb = k.astype(BF16)
        li_col = g_col[:, h:h + 1]
        li_row = g_row[h:h + 1, :]
        b_col = b_col_all[:, H_MLSTM + h:H_MLSTM + h + 1]
        b_row = b_row_all[H_MLSTM + h:H_MLSTM + h + 1, :]
        c_st = c_ref[h]
        n_st = n_ref[h]
        m_st = m_ref[h]

        d_log = jnp.where(causal, b_col - b_row + li_row, NEG_INF)
        m_inter = b_col + m_st
        m_t = jnp.maximum(m_inter, jnp.max(d_log, axis=-1, keepdims=True))
        w_inter = jnp.exp(m_inter - m_t)
        s = _dot_nt(qb, kb) * jnp.exp(d_log - m_t)
        num = w_inter * _dot(qb, c_st.astype(BF16)) + _dot(s.astype(BF16), v)
        den = w_inter * jnp.sum(q * n_st, axis=-1, keepdims=True) + jnp.sum(s, axis=-1, keepdims=True)
        hh = num / jnp.maximum(jnp.abs(den), jnp.exp(-m_t))
        hh = hh * lax.rsqrt(jnp.mean(hh * hh, axis=-1, keepdims=True) + RMS_EPS)
        hh = hh * ghead_ref[:, h * DV_MLSTM:(h + 1) * DV_MLSTM]
        og = og_ref[:, h * DV_MLSTM:(h + 1) * DV_MLSTM].astype(F32)
        o_ref[:, h * DV_MLSTM:(h + 1) * DV_MLSTM] = (jax.nn.sigmoid(og) * hh).astype(o_ref.dtype)

        b_last = b_col[L - 1:L, :]
        g_dec_row = b_last - b_row + li_row
        g_dec_col = b_last - b_col + li_col
        m_new = jnp.maximum(b_last + m_st, jnp.max(g_dec_row, axis=-1, keepdims=True))
        w_old = jnp.exp(b_last + m_st - m_new)
        kw = k * jnp.exp(g_dec_col - m_new)
        c_ref[h] = w_old * c_st + _dot(kw.T.astype(BF16), v)
        n_ref[h] = w_old * n_st + jnp.sum(kw, axis=0, keepdims=True)
        m_ref[h] = m_new


def _mlstm(z3, gates3, gates_t, gbias, gbias_t, conv_w, conv_b, g_head, tril, triu, layer):
    b, t, _ = z3.shape
    L = MLSTM_L
    nc = t // L
    nqk = 2 * H_MLSTM * DK_MLSTM
    nv = H_MLSTM * DV_MLSTM
    qk_blk = (2 * D_ATT) // nqk
    v_blk = (2 * D_ATT + nqk) // nv
    og_blk = v_blk + 1
    return pl.pallas_call(
        _mlstm_kernel,
        grid=(b, nc),
        in_specs=[
            pl.BlockSpec((None, L, nqk), lambda i, c: (i, c, qk_blk)),
            pl.BlockSpec((None, L, nv), lambda i, c: (i, c, v_blk)),
            pl.BlockSpec((None, L, nv), lambda i, c: (i, c, og_blk)),
            pl.BlockSpec((None, L, LANES), lambda i, c: (i, c, 0)),
            pl.BlockSpec((SUBLANES, L), lambda i, c: (0, i * nc + c)),
            pl.BlockSpec((None, 1, LANES), lambda i, c: (layer, 0, 0)),
            pl.BlockSpec((None, SUBLANES, 1), lambda i, c: (layer, 0, 0)),
            pl.BlockSpec((None, CONV_WIDTH, nqk), lambda i, c: (layer, 0, 0)),
            pl.BlockSpec((None, 1, nqk), lambda i, c: (layer, 0, 0)),
            pl.BlockSpec((None, 1, nv), lambda i, c: (layer, 0, 0)),
            pl.BlockSpec((L, L), lambda i, c: (0, 0)),
            pl.BlockSpec((L, L), lambda i, c: (0, 0)),
        ],
        out_specs=pl.BlockSpec((None, L, nv), lambda i, c: (i, c, 0)),
        out_shape=jax.ShapeDtypeStruct((b, t, nv), BF16),
        scratch_shapes=[
            pltpu.VMEM((L + SUBLANES, nqk), F32),
            pltpu.VMEM((L, nqk), F32),
            pltpu.VMEM((H_MLSTM, DK_MLSTM, DV_MLSTM), F32),
            pltpu.VMEM((H_MLSTM, 1, DK_MLSTM), F32),
            pltpu.VMEM((H_MLSTM, 1, 1), F32),
        ],
        compiler_params=_cparams(("parallel", "arbitrary")),
        name="mlstm",
    )(z3, z3, z3, gates3, gates_t, gbias, gbias_t, conv_w, conv_b, g_head, tril, triu)


def _outproj_kernel(h_ref, a_ref, m_ref, c_ref, w_ref, o_ref):
    na = H_MOBA * HEAD_DIM
    nm = H_MLSTM * DV_MLSTM
    acc = _dot(a_ref[...], w_ref[0:na, :])
    acc += _dot(m_ref[...], w_ref[na:na + nm, :])
    acc += _dot(c_ref[...], w_ref[na + nm:, :])
    o_ref[...] = h_ref[...] + acc


def _outproj(h, out_a, out_m, out_c, w_out, layer):
    m = h.shape[0]
    na, nm, ncol = out_a.shape[1], out_m.shape[1], out_c.shape[1]
    return pl.pallas_call(
        _outproj_kernel,
        grid=(m // TM,),
        in_specs=[
            pl.BlockSpec((TM, D_MODEL), lambda i: (i, 0)),
            pl.BlockSpec((TM, na), lambda i: (i, 0)),
            pl.BlockSpec((TM, nm), lambda i: (i, 0)),
            pl.BlockSpec((TM, ncol), lambda i: (i, 0)),
            pl.BlockSpec((None, na + nm + ncol, D_MODEL), lambda i: (layer, 0, 0)),
        ],
        out_specs=pl.BlockSpec((TM, D_MODEL), lambda i: (i, 0)),
        out_shape=jax.ShapeDtypeStruct((m, D_MODEL), F32),
        compiler_params=_cparams(("parallel",)),
        name="outproj",
    )(h, out_a, out_m, out_c, w_out)


def _ple_kernel(h_ref, g_ref, p_ref, wg_ref, wp_ref, gfin_ref, o_ref, *, final):
    x = h_ref[...]
    xn = _rms(x, g_ref[...]).astype(BF16)
    gate = jax.nn.sigmoid(_dot(xn, wg_ref[...]))
    proj = _dot(p_ref[...].astype(BF16), wp_ref[...])
    y = x + gate * proj
    if final:
        y = _rms(y, gfin_ref[...])
    o_ref[...] = y


def _ple(h, g, p2, w_gate, w_proj, g_final, layer, final):
    m = h.shape[0]
    return pl.pallas_call(
        functools.partial(_ple_kernel, final=final),
        grid=(m // TM,),
        in_specs=[
            pl.BlockSpec((TM, D_MODEL), lambda i: (i, 0)),
            pl.BlockSpec((None, 1, D_MODEL), lambda i: (layer, 0, 0)),
            pl.BlockSpec((None, TM, D_PLE), lambda i: (layer, i, 0)),
            pl.BlockSpec((None, D_MODEL, D_MODEL), lambda i: (layer, 0, 0)),
            pl.BlockSpec((None, D_PLE, D_MODEL), lambda i: (layer, 0, 0)),
            pl.BlockSpec((1, D_MODEL), lambda i: (0, 0)),
        ],
        out_specs=pl.BlockSpec((TM, D_MODEL), lambda i: (i, 0)),
        out_shape=jax.ShapeDtypeStruct((m, D_MODEL), F32),
        compiler_params=_cparams(("parallel",)),
        name="ple",
    )(h, g, p2, w_gate, w_proj, g_final)


def kernel(x, p, g_ffn1, w_up1, w_down1, g_mix, w_in, conv_w, conv_b, b_igate, b_fgate, g_head, w_out, g_ffn2, w_up2, w_down2, g_ple, w_ple_gate, w_ple_proj, g_final):
    b, t, d = x.shape
    depth = p.shape[0]
    m = b * t

    w_up1b, w_down1b = w_up1.astype(BF16), w_down1.astype(BF16)
    w_up2b, w_down2b = w_up2.astype(BF16), w_down2.astype(BF16)
    w_in_main = w_in[:, :, :D_Z].astype(BF16)
    w_in_gate = w_in[:, :, D_Z:]
    w_gate_col = jnp.pad(w_in_gate, ((0, 0), (0, 0), (0, LANES - N_GATES))).astype(BF16)
    w_gate_row = jnp.swapaxes(w_in_gate, 1, 2).astype(BF16)
    w_outb = w_out.astype(BF16)
    w_pgb, w_ppb = w_ple_gate.astype(BF16), w_ple_proj.astype(BF16)
    gate_bias = jnp.concatenate([b_igate, b_fgate], axis=-1)
    gbias_col = jnp.pad(gate_bias, ((0, 0), (0, LANES - N_GATES)))[:, None, :]
    gbias_row = gate_bias[:, :, None]
    row = lambda a: a[:, None, :]
    g_ffn1r, g_mixr, g_ffn2r, g_pler = row(g_ffn1), row(g_mix), row(g_ffn2), row(g_ple)
    conv_br, g_headr = row(conv_b), row(g_head)
    g_finalr = g_final[None, :]
    p2 = p.reshape(depth, m, D_PLE)

    slopes = 2.0 ** (-8.0 * jnp.arange(1, N_ALIBI + 1, dtype=F32) / N_ALIBI)
    slopes_moba, slopes_dil = slopes[0::2], slopes[1::2]
    counts = jnp.asarray(_dilated_window_counts(t))
    tri = np.tril(np.ones((MLSTM_L, MLSTM_L), np.float32))
    tril, triu = jnp.asarray(tri, BF16), jnp.asarray(tri.T, BF16)

    h = x.reshape(m, d)
    for i in range(depth):
        h = _ffn(h, g_ffn1r, w_up1b, w_down1b, i)
        z, gates, gates_t = _inproj(h, g_mixr, w_in_main, w_gate_col, w_gate_row, i)
        z3 = z.reshape(b, t, D_Z)
        out_a = _moba(z3, slopes_moba)
        out_c = _dilated(z3, slopes_dil, counts)
        out_m = _mlstm(z3, gates.reshape(b, t, LANES), gates_t, gbias_col, gbias_row,
                       conv_w, conv_br, g_headr, tril, triu, i)
        h = _outproj(h, out_a.reshape(m, -1), out_m.reshape(m, -1), out_c.reshape(m, -1), w_outb, i)
        h = _ffn(h, g_ffn2r, w_up2b, w_down2b, i)
        h = _ple(h, g_pler, p2, w_pgb, w_ppb, g_finalr, i, final=(i == depth - 1))
    return h.reshape(b, t, d)
```

```python
import functools

import numpy as np
import jax
import jax.numpy as jnp
from jax import lax
from jax.experimental import pallas as pl
from jax.experimental.pallas import tpu as pltpu

F32 = jnp.float32
BF16 = jnp.bfloat16

D_MODEL = 2048
DEPTH = 4
HEAD_DIM = 128
H_MOBA = 4
H_MLSTM = 4
H_DIL = 4
DK_MLSTM = 128
DV_MLSTM = 256
D_FF = 5632
D_PLE = 256
MOBA_BLOCK = 256
MOBA_TOPK = 3
DIL_PAIRS = ((128, 1), (512, 4), (2048, 16))
CONV_WIDTH = 4
RMS_EPS = 1e-6
LOG2_E = 1.4426950408889634
NEG_INF = -1e30
N_ALIBI = H_MOBA + H_DIL
D_ATT = 3 * H_MOBA * HEAD_DIM
D_Z = 2 * D_ATT + 2 * H_MLSTM * DK_MLSTM + 2 * H_MLSTM * DV_MLSTM
N_GATES = 2 * H_MLSTM

LANES = 128
SUBLANES = 8
VMEM_LIMIT_BYTES = 60 * 1024 * 1024

TM = 512
TM_BIG = 1024
TF = 512
TN_IN = 768
ATT_BLOCK = 256
MLSTM_L = 256


def _cparams(semantics):
    return pltpu.CompilerParams(dimension_semantics=semantics, vmem_limit_bytes=VMEM_LIMIT_BYTES)


def _rms(x, g):
    ms = jnp.mean(x * x, axis=-1, keepdims=True)
    return x * lax.rsqrt(ms + RMS_EPS) * g


def _dot(a, b):
    return jnp.dot(a, b, preferred_element_type=F32)


def _dot_nt(a, b):
    return lax.dot_general(a, b, (((1,), (1,)), ((), ())), preferred_element_type=F32)


def _cast_blocks(src_refs, dst_refs):
    for src, dst in zip(src_refs, dst_refs):
        dst[...] = src[...].astype(BF16)


def _ffn_kernel(*refs, n_cast):
    x_ref, g_ref, wg_ref, wu_ref, wd_ref = refs[:5]
    o_ref = refs[5 + n_cast]
    xn_ref = refs[-1]
    f = pl.program_id(1)

    @pl.when(f == 0)
    def _():
        xn_ref[...] = _rms(x_ref[...], g_ref[...]).astype(BF16)
        o_ref[...] = jnp.zeros_like(o_ref)

    xn = xn_ref[...]
    gate = _dot(xn, wg_ref[...])
    up = _dot(xn, wu_ref[...])
    act = (gate * jax.nn.sigmoid(gate) * up).astype(BF16)
    o_ref[...] += _dot(act, wd_ref[...])

    @pl.when(f == pl.num_programs(1) - 1)
    def _():
        o_ref[...] = x_ref[...] + 0.5 * o_ref[...]

    _cast_blocks(refs[5:5 + n_cast], refs[6 + n_cast:6 + 2 * n_cast])


def _ffn(h, g, w_up, w_down, layer, next_weights=None):
    m = h.shape[0]
    nm, nf = m // TM_BIG, D_FF // TF
    in_specs = [
        pl.BlockSpec((TM_BIG, D_MODEL), lambda i, f: (i, 0)),
        pl.BlockSpec((None, 1, D_MODEL), lambda i, f: (layer, 0, 0)),
        pl.BlockSpec((D_MODEL, TF), lambda i, f: (0, f)),
        pl.BlockSpec((D_MODEL, TF), lambda i, f: (0, f + nf)),
        pl.BlockSpec((TF, D_MODEL), lambda i, f: (f, 0)),
    ]
    out_specs = [pl.BlockSpec((TM_BIG, D_MODEL), lambda i, f: (i, 0))]
    out_shape = [jax.ShapeDtypeStruct((m, D_MODEL), F32)]
    inputs = [h, g, w_up, w_up, w_down]
    if next_weights is not None:
        src_up, src_down, nl = next_weights
        up_blk = (D_MODEL // nm, 2 * D_FF // nf)
        down_blk = (D_FF // nf, D_MODEL // nm)
        in_specs += [
            pl.BlockSpec((None, *up_blk), lambda i, f: (nl, i, f)),
            pl.BlockSpec((None, *down_blk), lambda i, f: (nl, f, i)),
        ]
        out_specs += [pl.BlockSpec(up_blk, lambda i, f: (i, f)), pl.BlockSpec(down_blk, lambda i, f: (f, i))]
        out_shape += [jax.ShapeDtypeStruct(src_up.shape[1:], BF16), jax.ShapeDtypeStruct(src_down.shape[1:], BF16)]
        inputs += [src_up, src_down]
    outs = pl.pallas_call(
        functools.partial(_ffn_kernel, n_cast=len(inputs) - 5),
        grid=(nm, nf),
        in_specs=in_specs,
        out_specs=out_specs,
        out_shape=out_shape,
        scratch_shapes=[pltpu.VMEM((TM_BIG, D_MODEL), BF16)],
        compiler_params=_cparams(("parallel", "arbitrary")),
        name="ffn",
    )(*inputs)
    return outs[0], tuple(outs[1:])


def _inproj_kernel(x_ref, g_ref, w_ref, wgate_ref, wgate_t_ref, *refs):
    n_cast = (len(refs) - 4) // 2
    z_ref, gates_ref, gates_t_ref = refs[n_cast:n_cast + 3]
    xn_ref = refs[-1]
    n = pl.program_id(1)

    @pl.when(n == 0)
    def _():
        xn = _rms(x_ref[...], g_ref[...]).astype(BF16)
        xn_ref[...] = xn
        gates_ref[...] = _dot(xn, wgate_ref[...])
        gates_t_ref[...] = _dot_nt(wgate_t_ref[...], xn)

    z_ref[...] = _dot(xn_ref[...], w_ref[...].astype(BF16)).astype(z_ref.dtype)
    _cast_blocks(refs[:n_cast], refs[n_cast + 3:2 * n_cast + 3])


def _inproj(h, g, w_in, w_gate, w_gate_t, layer, later_weights):
    m = h.shape[0]
    nm, nn = m // TM_BIG, D_Z // TN_IN
    in_specs = [
        pl.BlockSpec((TM_BIG, D_MODEL), lambda i, n: (i, 0)),
        pl.BlockSpec((None, 1, D_MODEL), lambda i, n: (layer, 0, 0)),
        pl.BlockSpec((None, D_MODEL, TN_IN), lambda i, n: (layer, 0, n)),
        pl.BlockSpec((None, D_MODEL, LANES), lambda i, n: (layer, 0, 0)),
        pl.BlockSpec((None, SUBLANES, D_MODEL), lambda i, n: (layer, 0, 0)),
    ]
    out_specs = [
        pl.BlockSpec((TM_BIG, TN_IN), lambda i, n: (i, n)),
        pl.BlockSpec((TM_BIG, LANES), lambda i, n: (i, 0)),
        pl.BlockSpec((SUBLANES, TM_BIG), lambda i, n: (0, i)),
    ]
    out_shape = [
        jax.ShapeDtypeStruct((m, D_Z), BF16),
        jax.ShapeDtypeStruct((m, LANES), F32),
        jax.ShapeDtypeStruct((SUBLANES, m), F32),
    ]
    for w in later_weights:
        blk = (w.shape[1] // nm, w.shape[2] // nn)
        in_specs.append(pl.BlockSpec((None, *blk), lambda i, n: (layer, i, n)))
        out_specs.append(pl.BlockSpec(blk, lambda i, n: (i, n)))
        out_shape.append(jax.ShapeDtypeStruct(w.shape[1:], BF16))
    outs = pl.pallas_call(
        _inproj_kernel,
        grid=(nm, nn),
        in_specs=in_specs,
        out_specs=out_specs,
        out_shape=out_shape,
        scratch_shapes=[pltpu.VMEM((TM_BIG, D_MODEL), BF16)],
        compiler_params=_cparams(("parallel", "arbitrary")),
        name="inproj",
    )(h, g, w_in, w_gate, w_gate_t, *later_weights)
    return outs[0], outs[1], outs[2], tuple(outs[3:])


def _head_cols(h):
    return slice(h * HEAD_DIM, (h + 1) * HEAD_DIM)


def _stage_values(v_ref, vt_ref, nblk):
    for h in range(vt_ref.shape[0]):
        for j in range(nblk):
            vj = v_ref[j * ATT_BLOCK:(j + 1) * ATT_BLOCK, _head_cols(h)].astype(F32)
            vt_ref[h, :, j * ATT_BLOCK:(j + 1) * ATT_BLOCK] = vj.T.astype(BF16)


def _local_distance():
    qi = lax.broadcasted_iota(jnp.int32, (ATT_BLOCK, ATT_BLOCK), 1)
    ki = lax.broadcasted_iota(jnp.int32, (ATT_BLOCK, ATT_BLOCK), 0)
    return (qi - ki).astype(F32)


def _fold_sublanes(x, op):
    return op(x.reshape(ATT_BLOCK // SUBLANES, SUBLANES, x.shape[-1]), axis=0)


def _softmax_pv(cc, score_fns, s_ref, p_ref, vt_ref, o_ref):
    nk = (cc + 1) * ATT_BLOCK
    blocks = [slice(j * ATT_BLOCK, (j + 1) * ATT_BLOCK) for j in range(cc + 1)]
    maxima = []
    for h, score_fn in enumerate(score_fns):
        m8 = None
        for j, blk in enumerate(blocks):
            s = score_fn(j)
            s_ref[h, blk, :] = s
            part = _fold_sublanes(s, jnp.max)
            m8 = part if m8 is None else jnp.maximum(m8, part)
        maxima.append(jnp.max(m8, axis=0, keepdims=True))
    for h, m in enumerate(maxima):
        l8 = None
        for blk in blocks:
            p = jnp.exp2(s_ref[h, blk, :] - m)
            p_ref[h, blk, :] = p.astype(BF16)
            part = _fold_sublanes(p, jnp.sum)
            l8 = part if l8 is None else l8 + part
        l = jnp.sum(l8, axis=0, keepdims=True)
        acc = _dot(vt_ref[h, :, 0:nk], p_ref[h, 0:nk, :])
        o_ref[:, h * HEAD_DIM:(h + 1) * HEAD_DIM] = (acc / l).T.astype(o_ref.dtype)


def _moba_kernel(slope_ref, q_ref, k_ref, v_ref, o_ref, kmean_ref, vt_ref, s_ref, p_ref, *, nblk):
    c = pl.program_id(1)
    nh = vt_ref.shape[0]
    qk_scale = HEAD_DIM ** -0.5 * LOG2_E

    @pl.when(c == 0)
    def _():
        for h in range(nh):
            for j in range(nblk):
                kj = k_ref[j * ATT_BLOCK:(j + 1) * ATT_BLOCK, _head_cols(h)].astype(F32)
                kmean_ref[h, j:j + 1, :] = jnp.mean(kj, axis=0, keepdims=True)
        _stage_values(v_ref, vt_ref, nblk)

    def head_scores(cc, h, dist):
        q = q_ref[:, _head_cols(h)]
        slope2 = slope_ref[h] * LOG2_E
        alibi = slope2 * dist
        if cc > MOBA_TOPK:
            km = kmean_ref[h]
            km_hi = km.astype(BF16)
            km_lo = (km - km_hi.astype(F32)).astype(BF16)
            gate = _dot_nt(km_hi, q) + _dot_nt(km_lo, q)
            g = [gate[n:n + 1, :] for n in range(cc)]
            sel = []
            for n in range(cc):
                n_ahead = jnp.zeros_like(g[n])
                for mth in range(cc):
                    if mth != n:
                        ahead = (g[mth] >= g[n]) if mth < n else (g[mth] > g[n])
                        n_ahead = n_ahead + jnp.where(ahead, 1.0, 0.0)
                sel.append(jnp.where(n_ahead < MOBA_TOPK, 0.0, NEG_INF))
        else:
            sel = None

        def score(j):
            kj = k_ref[j * ATT_BLOCK:(j + 1) * ATT_BLOCK, _head_cols(h)]
            s = _dot_nt(kj, q) * qk_scale - alibi
            if j == cc:
                return jnp.where(dist >= 0, s, NEG_INF)
            s = s - slope2 * float((cc - j) * ATT_BLOCK)
            return s if sel is None else s + sel[j]

        return score

    def branch(cc):
        dist = _local_distance()
        _softmax_pv(cc, [head_scores(cc, h, dist) for h in range(nh)], s_ref, p_ref, vt_ref, o_ref)

    for cc in range(nblk):
        pl.when(c == cc)(functools.partial(branch, cc))


def _dil_kernel(slope_ref, q_ref, k_ref, v_ref, lcnt_ref, o_ref, vt_ref, s_ref, p_ref, *, nblk):
    c = pl.program_id(1)
    nh = vt_ref.shape[0]
    qk_scale = HEAD_DIM ** -0.5 * LOG2_E

    @pl.when(c == 0)
    def _():
        _stage_values(v_ref, vt_ref, nblk)

    def head_scores(cc, h, dist):
        q = q_ref[:, _head_cols(h)]
        slope2 = slope_ref[h] * LOG2_E
        alibi = slope2 * dist

        def score(j):
            kj = k_ref[j * ATT_BLOCK:(j + 1) * ATT_BLOCK, _head_cols(h)]
            s = _dot_nt(kj, q) * qk_scale - alibi
            if j < cc:
                s = s - slope2 * float((cc - j) * ATT_BLOCK)
            return s + lcnt_ref[cc - j]

        return score

    def branch(cc):
        dist = _local_distance()
        _softmax_pv(cc, [head_scores(cc, h, dist) for h in range(nh)], s_ref, p_ref, vt_ref, o_ref)

    for cc in range(nblk):
        pl.when(c == cc)(functools.partial(branch, cc))


def _dilated_window_log_counts(seq):
    nblk = seq // ATT_BLOCK
    kl = np.arange(ATT_BLOCK)[None, :, None]
    ql = np.arange(ATT_BLOCK)[None, None, :]
    d = np.arange(nblk)[:, None, None] * ATT_BLOCK + ql - kl
    cnt = np.zeros(d.shape, np.float64)
    for window, dil in DIL_PAIRS:
        cnt += (d >= 0) & (d <= window) & (d % dil == 0)
    return np.where(cnt > 0, np.log2(np.maximum(cnt, 1.0)), NEG_INF).astype(np.float32)


def _attention(kernel, z3, slopes, col0, extra_inputs, extra_specs, extra_scratch, name):
    b, t, _ = z3.shape
    nblk = t // ATT_BLOCK
    nh = H_MOBA
    width = nh * HEAD_DIM
    cb = col0 // width
    scratch = [
        *extra_scratch,
        pltpu.VMEM((nh, HEAD_DIM, t), BF16),
        pltpu.VMEM((nh, t, ATT_BLOCK), F32),
        pltpu.VMEM((nh, t, ATT_BLOCK), BF16),
    ]
    return pl.pallas_call(
        functools.partial(kernel, nblk=nblk),
        grid=(b, nblk),
        in_specs=[
            pl.BlockSpec(memory_space=pltpu.SMEM),
            pl.BlockSpec((None, ATT_BLOCK, width), lambda i, c: (i, c, cb)),
            pl.BlockSpec((None, t, width), lambda i, c: (i, 0, cb + 1)),
            pl.BlockSpec((None, t, width), lambda i, c: (i, 0, cb + 2)),
            *extra_specs,
        ],
        out_specs=pl.BlockSpec((None, ATT_BLOCK, width), lambda i, c: (i, c, 0)),
        out_shape=jax.ShapeDtypeStruct((b, t, width), BF16),
        scratch_shapes=scratch,
        compiler_params=_cparams(("parallel", "arbitrary")),
        name=name,
    )(slopes, z3, z3, z3, *extra_inputs)


def _moba(z3, slopes):
    nblk = z3.shape[1] // ATT_BLOCK
    scratch = [pltpu.VMEM((H_MOBA, nblk, HEAD_DIM), F32)]
    return _attention(_moba_kernel, z3, slopes, 0, (), (), scratch, "moba")


def _dilated(z3, slopes, log_counts):
    nblk = z3.shape[1] // ATT_BLOCK
    spec = pl.BlockSpec((nblk, ATT_BLOCK, ATT_BLOCK), lambda i, c: (0, 0, 0))
    return _attention(_dil_kernel, z3, slopes, D_ATT, (log_counts,), (spec,), [], "dilated")


def _log_sigmoid(x):
    return jnp.minimum(x, 0.0) - jnp.log(1.0 + jnp.exp(-jnp.abs(x)))


def _split3(x):
    hi = x.astype(BF16)
    r1 = x - hi.astype(F32)
    mid = r1.astype(BF16)
    lo = (r1 - mid.astype(F32)).astype(BF16)
    return hi, mid, lo


def _mlstm_kernel(qk_ref, v_ref, og_ref, gates_ref, gates_t_ref, gbias_ref, gbias_t_ref, cw_ref, cb_ref,
                  ghead_ref, tril_ref, triu_ref, o_ref, ext_ref, qk_act_ref, c_ref, n_ref, m_ref):
    L = MLSTM_L
    nqk = H_MLSTM * DK_MLSTM
    chunk = pl.program_id(1)

    @pl.when(chunk == 0)
    def _():
        c_ref[...] = jnp.zeros_like(c_ref)
        n_ref[...] = jnp.zeros_like(n_ref)
        m_ref[...] = jnp.zeros_like(m_ref)
        ext_ref[0:SUBLANES, :] = jnp.zeros((SUBLANES, 2 * nqk), F32)

    u = qk_ref[...].astype(F32)
    ext_ref[SUBLANES:SUBLANES + L, :] = u
    cw = cw_ref[...]
    y = cb_ref[...] + cw[CONV_WIDTH - 1:CONV_WIDTH, :] * u
    for back in range(1, CONV_WIDTH):
        w_row = cw[CONV_WIDTH - 1 - back:CONV_WIDTH - back, :]
        y = y + w_row * ext_ref[pl.ds(SUBLANES - back, L), :]
    ext_ref[0:SUBLANES, :] = u[L - SUBLANES:L, :]
    qk_act_ref[...] = y * jax.nn.sigmoid(y)

    g_col = gates_ref[...] + gbias_ref[...]
    g_row = gates_t_ref[...] + gbias_t_ref[...]
    lf_col = _log_sigmoid(g_col)
    lf_row = _log_sigmoid(g_row)
    b_col_all = sum(_dot(tril_ref[...], t) for t in _split3(lf_col))
    b_row_all = sum(_dot(t, triu_ref[...]) for t in _split3(lf_row))

    ti = lax.broadcasted_iota(jnp.int32, (L, L), 0)
    si = lax.broadcasted_iota(jnp.int32, (L, L), 1)
    causal = si <= ti

    for h in range(H_MLSTM):
        q = qk_act_ref[:, h * DK_MLSTM:(h + 1) * DK_MLSTM]
        k = qk_act_ref[:, nqk + h * DK_MLSTM:nqk + (h + 1) * DK_MLSTM] * (DK_MLSTM ** -0.5)
        v = v_ref[:, h * DV_MLSTM:(h + 1) * DV_MLSTM]
        qb = q.astype(BF16)
        kb = k.astype(BF16)
        li_col = g_col[:, h:h + 1]
        li_row = g_row[h:h + 1, :]
        b_col = b_col_all[:, H_MLSTM + h:H_MLSTM + h + 1]
        b_row = b_row_all[H_MLSTM + h:H_MLSTM + h + 1, :]
        c_st = c_ref[h]
        n_st = n_ref[h]
        m_st = m_ref[h]

        d_log = jnp.where(causal, b_col - b_row + li_row, NEG_INF)
        m_inter = b_col + m_st
        m_t = jnp.maximum(m_inter, jnp.max(d_log, axis=-1, keepdims=True))
        w_inter = jnp.exp(m_inter - m_t)
        s = _dot_nt(qb, kb) * jnp.exp(d_log - m_t)
        num = w_inter * _dot(qb, c_st.astype(BF16)) + _dot(s.astype(BF16), v)
        den = w_inter * jnp.sum(q * n_st, axis=-1, keepdims=True) + jnp.sum(s, axis=-1, keepdims=True)
        hh = num / jnp.maximum(jnp.abs(den), jnp.exp(-m_t))
        hh = hh * lax.rsqrt(jnp.mean(hh * hh, axis=-1, keepdims=True) + RMS_EPS)
        hh = hh * ghead_ref[:, h * DV_MLSTM:(h + 1) * DV_MLSTM]
        og = og_ref[:, h * DV_MLSTM:(h + 1) * DV_MLSTM].astype(F32)
        o_ref[:, h * DV_MLSTM:(h + 1) * DV_MLSTM] = (jax.nn.sigmoid(og) * hh).astype(o_ref.dtype)

        b_last = b_col[L - 1:L, :]
        g_dec_row = b_last - b_row + li_row
        g_dec_col = b_last - b_col + li_col
        m_new = jnp.maximum(b_last + m_st, jnp.max(g_dec_row, axis=-1, keepdims=True))
        w_old = jnp.exp(b_last + m_st - m_new)
        kw = k * jnp.exp(g_dec_col - m_new)
        c_ref[h] = w_old * c_st + _dot(kw.T.astype(BF16), v)
        n_ref[h] = w_old * n_st + jnp.sum(kw, axis=0, keepdims=True)
        m_ref[h] = m_new


def _mlstm(z3, gates3, gates_t, gbias, gbias_t, conv_w, conv_b, g_head, tril, triu, layer):
    b, t, _ = z3.shape
    L = MLSTM_L
    nc = t // L
    nqk = 2 * H_MLSTM * DK_MLSTM
    nv = H_MLSTM * DV_MLSTM
    qk_blk = (2 * D_ATT) // nqk
    v_blk = (2 * D_ATT + nqk) // nv
    og_blk = v_blk + 1
    return pl.pallas_call(
        _mlstm_kernel,
        grid=(b, nc),
        in_specs=[
            pl.BlockSpec((None, L, nqk), lambda i, c: (i, c, qk_blk)),
            pl.BlockSpec((None, L, nv), lambda i, c: (i, c, v_blk)),
            pl.BlockSpec((None, L, nv), lambda i, c: (i, c, og_blk)),
            pl.BlockSpec((None, L, LANES), lambda i, c: (i, c, 0)),
            pl.BlockSpec((SUBLANES, L), lambda i, c: (0, i * nc + c)),
            pl.BlockSpec((None, 1, LANES), lambda i, c: (layer, 0, 0)),
            pl.BlockSpec((None, SUBLANES, 1), lambda i, c: (layer, 0, 0)),
            pl.BlockSpec((None, CONV_WIDTH, nqk), lambda i, c: (layer, 0, 0)),
            pl.BlockSpec((None, 1, nqk), lambda i, c: (layer, 0, 0)),
            pl.BlockSpec((None, 1, nv), lambda i, c: (layer, 0, 0)),
            pl.BlockSpec((L, L), lambda i, c: (0, 0)),
            pl.BlockSpec((L, L), lambda i, c: (0, 0)),
        ],
        out_specs=pl.BlockSpec((None, L, nv), lambda i, c: (i, c, 0)),
        out_shape=jax.ShapeDtypeStruct((b, t, nv), BF16),
        scratch_shapes=[
            pltpu.VMEM((L + SUBLANES, nqk), F32),
            pltpu.VMEM((L, nqk), F32),
            pltpu.VMEM((H_MLSTM, DK_MLSTM, DV_MLSTM), F32),
            pltpu.VMEM((H_MLSTM, 1, DK_MLSTM), F32),
            pltpu.VMEM((H_MLSTM, 1, 1), F32),
        ],
        compiler_params=_cparams(("parallel", "arbitrary")),
        name="mlstm",
    )(z3, z3, z3, gates3, gates_t, gbias, gbias_t, conv_w, conv_b, g_head, tril, triu)


def _outproj_kernel(h_ref, a_ref, m_ref, c_ref, w_ref, o_ref):
    na = H_MOBA * HEAD_DIM
    nm = H_MLSTM * DV_MLSTM
    acc = _dot(a_ref[...], w_ref[0:na, :])
    acc += _dot(m_ref[...], w_ref[na:na + nm, :])
    acc += _dot(c_ref[...], w_ref[na + nm:, :])
    o_ref[...] = h_ref[...] + acc


def _outproj(h, out_a, out_m, out_c, w_out):
    m = h.shape[0]
    na, nm, ncol = out_a.shape[1], out_m.shape[1], out_c.shape[1]
    return pl.pallas_call(
        _outproj_kernel,
        grid=(m // TM,),
        in_specs=[
            pl.BlockSpec((TM, D_MODEL), lambda i: (i, 0)),
            pl.BlockSpec((TM, na), lambda i: (i, 0)),
            pl.BlockSpec((TM, nm), lambda i: (i, 0)),
            pl.BlockSpec((TM, ncol), lambda i: (i, 0)),
            pl.BlockSpec((na + nm + ncol, D_MODEL), lambda i: (0, 0)),
        ],
        out_specs=pl.BlockSpec((TM, D_MODEL), lambda i: (i, 0)),
        out_shape=jax.ShapeDtypeStruct((m, D_MODEL), F32),
        compiler_params=_cparams(("parallel",)),
        name="outproj",
    )(h, out_a, out_m, out_c, w_out)


def _ple_kernel(h_ref, g_ref, p_ref, wg_ref, wp_ref, gfin_ref, o_ref, *, final):
    x = h_ref[...]
    xn = _rms(x, g_ref[...]).astype(BF16)
    gate = jax.nn.sigmoid(_dot(xn, wg_ref[...]))
    proj = _dot(p_ref[...].astype(BF16), wp_ref[...])
    y = x + gate * proj
    if final:
        y = _rms(y, gfin_ref[...])
    o_ref[...] = y


def _ple(h, g, p2, w_gate, w_proj, g_final, layer, final):
    m = h.shape[0]
    return pl.pallas_call(
        functools.partial(_ple_kernel, final=final),
        grid=(m // TM,),
        in_specs=[
            pl.BlockSpec((TM, D_MODEL), lambda i: (i, 0)),
            pl.BlockSpec((None, 1, D_MODEL), lambda i: (layer, 0, 0)),
            pl.BlockSpec((None, TM, D_PLE), lambda i: (layer, i, 0)),
            pl.BlockSpec((D_MODEL, D_MODEL), lambda i: (0, 0)),
            pl.BlockSpec((D_PLE, D_MODEL), lambda i: (0, 0)),
            pl.BlockSpec((1, D_MODEL), lambda i: (0, 0)),
        ],
        out_specs=pl.BlockSpec((TM, D_MODEL), lambda i: (i, 0)),
        out_shape=jax.ShapeDtypeStruct((m, D_MODEL), F32),
        compiler_params=_cparams(("parallel",)),
        name="ple",
    )(h, g, p2, w_gate, w_proj, g_final)


def kernel(x, p, g_ffn1, w_up1, w_down1, g_mix, w_in, conv_w, conv_b, b_igate, b_fgate, g_head, w_out, g_ffn2, w_up2, w_down2, g_ple, w_ple_gate, w_ple_proj, g_final):
    b, t, d = x.shape
    depth = p.shape[0]
    m = b * t

    ffn_w = (w_up1[0].astype(BF16), w_down1[0].astype(BF16))
    w_in_gate = w_in[:, :, D_Z:]
    w_gate_col = jnp.pad(w_in_gate, ((0, 0), (0, 0), (0, LANES - N_GATES))).astype(BF16)
    w_gate_row = jnp.swapaxes(w_in_gate, 1, 2).astype(BF16)
    gate_bias = jnp.concatenate([b_igate, b_fgate], axis=-1)
    gbias_col = jnp.pad(gate_bias, ((0, 0), (0, LANES - N_GATES)))[:, None, :]
    gbias_row = gate_bias[:, :, None]
    row = lambda a: a[:, None, :]
    g_ffn1r, g_mixr, g_ffn2r, g_pler = row(g_ffn1), row(g_mix), row(g_ffn2), row(g_ple)
    conv_br, g_headr = row(conv_b), row(g_head)
    g_finalr = g_final[None, :]
    p2 = p.reshape(depth, m, D_PLE)

    slopes = 2.0 ** (-8.0 * jnp.arange(1, N_ALIBI + 1, dtype=F32) / N_ALIBI)
    slopes_moba, slopes_dil = slopes[0::2], slopes[1::2]
    log_counts = jnp.asarray(_dilated_window_log_counts(t))
    tri = np.tril(np.ones((MLSTM_L, MLSTM_L), np.float32))
    tril, triu = jnp.asarray(tri, BF16), jnp.asarray(tri.T, BF16)

    h = x.reshape(m, d)
    for i in range(depth):
        h, ffn_w = _ffn(h, g_ffn1r, *ffn_w, i, next_weights=(w_up2, w_down2, i))
        z, gates, gates_t, (w_outb, w_pgb, w_ppb) = _inproj(
            h, g_mixr, w_in, w_gate_col, w_gate_row, i, (w_out, w_ple_gate, w_ple_proj))
        z3 = z.reshape(b, t, D_Z)
        out_a = _moba(z3, slopes_moba)
        out_c = _dilated(z3, slopes_dil, log_counts)
        out_m = _mlstm(z3, gates.reshape(b, t, LANES), gates_t, gbias_col, gbias_row,
                       conv_w, conv_br, g_headr, tril, triu, i)
        h = _outproj(h, out_a.reshape(m, -1), out_m.reshape(m, -1), out_c.reshape(m, -1), w_outb)
        upcoming = (w_up1, w_down1, i + 1) if i + 1 < depth else None
        h, ffn_w = _ffn(h, g_ffn2r, *ffn_w, i, next_weights=upcoming)
        h = _ple(h, g_pler, p2, w_pgb, w_ppb, g_finalr, i, final=(i == depth - 1))
    return h.reshape(b, t, d)
```

```python
import functools

import numpy as np
import jax
import jax.numpy as jnp
from jax import lax
from jax.experimental import pallas as pl
from jax.experimental.pallas import tpu as pltpu

F32 = jnp.float32
BF16 = jnp.bfloat16

D_MODEL = 2048
DEPTH = 4
HEAD_DIM = 128
H_MOBA = 4
H_MLSTM = 4
H_DIL = 4
DK_MLSTM = 128
DV_MLSTM = 256
D_FF = 5632
D_PLE = 256
MOBA_BLOCK = 256
MOBA_TOPK = 3
DIL_PAIRS = ((128, 1), (512, 4), (2048, 16))
CONV_WIDTH = 4
RMS_EPS = 1e-6
LOG2_E = 1.4426950408889634
NEG_INF = -1e30
N_ALIBI = H_MOBA + H_DIL
D_ATT = 3 * H_MOBA * HEAD_DIM
D_Z = 2 * D_ATT + 2 * H_MLSTM * DK_MLSTM + 2 * H_MLSTM * DV_MLSTM
N_GATES = 2 * H_MLSTM

LANES = 128
SUBLANES = 8
VMEM_LIMIT_BYTES = 60 * 1024 * 1024

TM = 512
TM_BIG = 1024
TF = 512
TN_IN = 768
ATT_BLOCK = 256
MLSTM_L = 256


def _cparams(semantics):
    return pltpu.CompilerParams(dimension_semantics=semantics, vmem_limit_bytes=VMEM_LIMIT_BYTES)


def _rms(x, g):
    ms = jnp.mean(x * x, axis=-1, keepdims=True)
    return x * lax.rsqrt(ms + RMS_EPS) * g


def _dot(a, b):
    return jnp.dot(a, b, preferred_element_type=F32)


def _dot_nt(a, b):
    return lax.dot_general(a, b, (((1,), (1,)), ((), ())), preferred_element_type=F32)


def _cast_blocks(src_refs, dst_refs):
    for src, dst in zip(src_refs, dst_refs):
        dst[...] = src[...].astype(BF16)


def _ffn_kernel(*refs, n_cast):
    x_ref, g_ref, wg_ref, wu_ref, wd_ref = refs[:5]
    o_ref = refs[5 + n_cast]
    xn_ref = refs[-1]
    f = pl.program_id(1)

    @pl.when(f == 0)
    def _():
        xn_ref[...] = _rms(x_ref[...], g_ref[...]).astype(BF16)
        o_ref[...] = jnp.zeros_like(o_ref)

    xn = xn_ref[...]
    gate = _dot(xn, wg_ref[...])
    up = _dot(xn, wu_ref[...])
    act = (gate * jax.nn.sigmoid(gate) * up).astype(BF16)
    o_ref[...] += _dot(act, wd_ref[...])
    _cast_blocks(refs[5:5 + n_cast], refs[6 + n_cast:6 + 2 * n_cast])

    @pl.when(f == pl.num_programs(1) - 1)
    def _():
        o_ref[...] = x_ref[...] + 0.5 * o_ref[...]


def _ffn(h, g, w_up, w_down, layer, next_weights=None):
    m = h.shape[0]
    nm, nf = m // TM_BIG, D_FF // TF
    in_specs = [
        pl.BlockSpec((TM_BIG, D_MODEL), lambda i, f: (i, 0)),
        pl.BlockSpec((None, 1, D_MODEL), lambda i, f: (layer, 0, 0)),
        pl.BlockSpec((D_MODEL, TF), lambda i, f: (0, f)),
        pl.BlockSpec((D_MODEL, TF), lambda i, f: (0, f + nf)),
        pl.BlockSpec((TF, D_MODEL), lambda i, f: (f, 0)),
    ]
    out_specs = [pl.BlockSpec((TM_BIG, D_MODEL), lambda i, f: (i, 0))]
    out_shape = [jax.ShapeDtypeStruct((m, D_MODEL), F32)]
    inputs = [h, g, w_up, w_up, w_down]
    if next_weights is not None:
        src_up, src_down, nl = next_weights
        up_blk = (D_MODEL // nm, 2 * D_FF // nf)
        down_blk = (D_FF // nf, D_MODEL // nm)
        in_specs += [
            pl.BlockSpec((None, *up_blk), lambda i, f: (nl, i, f)),
            pl.BlockSpec((None, *down_blk), lambda i, f: (nl, f, i)),
        ]
        out_specs += [pl.BlockSpec(up_blk, lambda i, f: (i, f)), pl.BlockSpec(down_blk, lambda i, f: (f, i))]
        out_shape += [jax.ShapeDtypeStruct(src_up.shape[1:], BF16), jax.ShapeDtypeStruct(src_down.shape[1:], BF16)]
        inputs += [src_up, src_down]
    outs = pl.pallas_call(
        functools.partial(_ffn_kernel, n_cast=len(inputs) - 5),
        grid=(nm, nf),
        in_specs=in_specs,
        out_specs=out_specs,
        out_shape=out_shape,
        scratch_shapes=[pltpu.VMEM((TM_BIG, D_MODEL), BF16)],
        compiler_params=_cparams(("parallel", "arbitrary")),
        name="ffn",
    )(*inputs)
    return outs[0], tuple(outs[1:])


def _inproj_kernel(x_ref, g_ref, w_ref, wgate_ref, wgate_t_ref, *refs):
    n_cast = (len(refs) - 4) // 2
    z_ref, gates_ref, gates_t_ref = refs[n_cast:n_cast + 3]
    xn_ref = refs[-1]
    n = pl.program_id(1)

    @pl.when(n == 0)
    def _():
        xn = _rms(x_ref[...], g_ref[...]).astype(BF16)
        xn_ref[...] = xn
        gates_ref[...] = _dot(xn, wgate_ref[...])
        gates_t_ref[...] = _dot_nt(wgate_t_ref[...], xn)

    z_ref[...] = _dot_nt(xn_ref[...], w_ref[...]).astype(z_ref.dtype)
    _cast_blocks(refs[:n_cast], refs[n_cast + 3:2 * n_cast + 3])


def _inproj(h, g, w_in_t, w_gate, w_gate_t, layer, later_weights):
    m = h.shape[0]
    nm, nn = m // TM_BIG, D_Z // TN_IN
    in_specs = [
        pl.BlockSpec((TM_BIG, D_MODEL), lambda i, n: (i, 0)),
        pl.BlockSpec((None, 1, D_MODEL), lambda i, n: (layer, 0, 0)),
        pl.BlockSpec((TN_IN, D_MODEL), lambda i, n: (n, 0)),
        pl.BlockSpec((None, D_MODEL, LANES), lambda i, n: (layer, 0, 0)),
        pl.BlockSpec((None, SUBLANES, D_MODEL), lambda i, n: (layer, 0, 0)),
    ]
    out_specs = [
        pl.BlockSpec((TM_BIG, TN_IN), lambda i, n: (i, n)),
        pl.BlockSpec((TM_BIG, LANES), lambda i, n: (i, 0)),
        pl.BlockSpec((SUBLANES, TM_BIG), lambda i, n: (0, i)),
    ]
    out_shape = [
        jax.ShapeDtypeStruct((m, D_Z), BF16),
        jax.ShapeDtypeStruct((m, LANES), F32),
        jax.ShapeDtypeStruct((SUBLANES, m), F32),
    ]
    for w in later_weights:
        blk = (w.shape[1] // nm, w.shape[2] // nn)
        in_specs.append(pl.BlockSpec((None, *blk), lambda i, n: (layer, i, n)))
        out_specs.append(pl.BlockSpec(blk, lambda i, n: (i, n)))
        out_shape.append(jax.ShapeDtypeStruct(w.shape[1:], BF16))
    outs = pl.pallas_call(
        _inproj_kernel,
        grid=(nm, nn),
        in_specs=in_specs,
        out_specs=out_specs,
        out_shape=out_shape,
        scratch_shapes=[pltpu.VMEM((TM_BIG, D_MODEL), BF16)],
        compiler_params=_cparams(("parallel", "arbitrary")),
        name="inproj",
    )(h, g, w_in_t, w_gate, w_gate_t, *later_weights)
    return outs[0], outs[1], outs[2], tuple(outs[3:])


def _head_cols(h):
    return slice(h * HEAD_DIM, (h + 1) * HEAD_DIM)


def _stage_values(v_ref, vt_ref, nblk):
    for h in range(vt_ref.shape[0]):
        for j in range(nblk):
            vj = v_ref[j * ATT_BLOCK:(j + 1) * ATT_BLOCK, _head_cols(h)].astype(F32)
            vt_ref[h, :, j * ATT_BLOCK:(j + 1) * ATT_BLOCK] = vj.T.astype(BF16)


def _local_distance():
    qi = lax.broadcasted_iota(jnp.int32, (ATT_BLOCK, ATT_BLOCK), 1)
    ki = lax.broadcasted_iota(jnp.int32, (ATT_BLOCK, ATT_BLOCK), 0)
    return (qi - ki).astype(F32)


def _fold_sublanes(x, op):
    return op(x.reshape(ATT_BLOCK // SUBLANES, SUBLANES, x.shape[-1]), axis=0)


def _softmax_pv(cc, score_fns, s_ref, p_ref, vt_ref, o_ref):
    nk = (cc + 1) * ATT_BLOCK
    blocks = [slice(j * ATT_BLOCK, (j + 1) * ATT_BLOCK) for j in range(cc + 1)]
    maxima = []
    for h, score_fn in enumerate(score_fns):
        m8 = None
        for j, blk in enumerate(blocks):
            s = score_fn(j)
            s_ref[h, blk, :] = s
            part = _fold_sublanes(s, jnp.max)
            m8 = part if m8 is None else jnp.maximum(m8, part)
        maxima.append(jnp.max(m8, axis=0, keepdims=True))
    for h, m in enumerate(maxima):
        l8 = None
        for blk in blocks:
            p = jnp.exp2(s_ref[h, blk, :] - m)
            p_ref[h, blk, :] = p.astype(BF16)
            part = _fold_sublanes(p, jnp.sum)
            l8 = part if l8 is None else l8 + part
        l = jnp.sum(l8, axis=0, keepdims=True)
        acc = _dot(vt_ref[h, :, 0:nk], p_ref[h, 0:nk, :])
        o_ref[:, h * HEAD_DIM:(h + 1) * HEAD_DIM] = (acc / l).T.astype(o_ref.dtype)


def _attn_kernel(slope_ref, q_ref, k_ref, v_ref, mask_ref, o_ref, vt_ref, s_ref, p_ref, bias_ref, kmean_ref, *,
                 nblk, topk):
    c = pl.program_id(1)
    nh = vt_ref.shape[0]
    qk_scale = HEAD_DIM ** -0.5 * LOG2_E

    @pl.when(c == 0)
    def _():
        _stage_values(v_ref, vt_ref, nblk)
        dist = _local_distance()
        for h in range(nh):
            slope2 = slope_ref[h] * LOG2_E
            for diff in range(nblk):
                bias_ref[h, diff] = mask_ref[diff] - slope2 * (dist + float(diff * ATT_BLOCK))
            if topk is not None:
                for j in range(nblk):
                    kj = k_ref[j * ATT_BLOCK:(j + 1) * ATT_BLOCK, _head_cols(h)].astype(F32)
                    kmean_ref[h, j:j + 1, :] = jnp.mean(kj, axis=0, keepdims=True)

    def head_scores(cc, h):
        q = q_ref[:, _head_cols(h)]
        if topk is not None and cc > topk:
            km = kmean_ref[h]
            km_hi = km.astype(BF16)
            km_lo = (km - km_hi.astype(F32)).astype(BF16)
            gate = _dot_nt(km_hi, q) + _dot_nt(km_lo, q)
            g = [gate[n:n + 1, :] for n in range(cc)]
            sel = []
            for n in range(cc):
                n_ahead = jnp.zeros_like(g[n])
                for mth in range(cc):
                    if mth != n:
                        ahead = (g[mth] >= g[n]) if mth < n else (g[mth] > g[n])
                        n_ahead = n_ahead + jnp.where(ahead, 1.0, 0.0)
                sel.append(jnp.where(n_ahead < topk, 0.0, NEG_INF))
        else:
            sel = None

        def score(j):
            kj = k_ref[j * ATT_BLOCK:(j + 1) * ATT_BLOCK, _head_cols(h)]
            s = _dot_nt(kj, q) * qk_scale + bias_ref[h, cc - j]
            return s if (sel is None or j == cc) else s + sel[j]

        return score

    def branch(cc):
        _softmax_pv(cc, [head_scores(cc, h) for h in range(nh)], s_ref, p_ref, vt_ref, o_ref)

    for cc in range(nblk):
        pl.when(c == cc)(functools.partial(branch, cc))


def _block_distances(seq):
    nblk = seq // ATT_BLOCK
    kl = np.arange(ATT_BLOCK)[None, :, None]
    ql = np.arange(ATT_BLOCK)[None, None, :]
    return np.arange(nblk)[:, None, None] * ATT_BLOCK + ql - kl


def _causal_mask(seq):
    return np.where(_block_distances(seq) >= 0, 0.0, NEG_INF).astype(np.float32)


def _dilated_window_log_counts(seq):
    d = _block_distances(seq)
    cnt = np.zeros(d.shape, np.float64)
    for window, dil in DIL_PAIRS:
        cnt += (d >= 0) & (d <= window) & (d % dil == 0)
    return np.where(cnt > 0, np.log2(np.maximum(cnt, 1.0)), NEG_INF).astype(np.float32)


def _attention(z3, slopes, mask, col0, topk, name):
    b, t, _ = z3.shape
    nblk = t // ATT_BLOCK
    nh = slopes.shape[0]
    width = nh * HEAD_DIM
    cb = col0 // width
    return pl.pallas_call(
        functools.partial(_attn_kernel, nblk=nblk, topk=topk),
        grid=(b, nblk),
        in_specs=[
            pl.BlockSpec(memory_space=pltpu.SMEM),
            pl.BlockSpec((None, ATT_BLOCK, width), lambda i, c: (i, c, cb)),
            pl.BlockSpec((None, t, width), lambda i, c: (i, 0, cb + 1)),
            pl.BlockSpec((None, t, width), lambda i, c: (i, 0, cb + 2)),
            pl.BlockSpec((nblk, ATT_BLOCK, ATT_BLOCK), lambda i, c: (0, 0, 0)),
        ],
        out_specs=pl.BlockSpec((None, ATT_BLOCK, width), lambda i, c: (i, c, 0)),
        out_shape=jax.ShapeDtypeStruct((b, t, width), BF16),
        scratch_shapes=[
            pltpu.VMEM((nh, HEAD_DIM, t), BF16),
            pltpu.VMEM((nh, t, ATT_BLOCK), F32),
            pltpu.VMEM((nh, t, ATT_BLOCK), BF16),
            pltpu.VMEM((nh, nblk, ATT_BLOCK, ATT_BLOCK), F32),
            pltpu.VMEM((nh, nblk, HEAD_DIM), F32),
        ],
        compiler_params=_cparams(("parallel", "arbitrary")),
        name=name,
    )(slopes, z3, z3, z3, mask)


def _log_sigmoid(x):
    return jnp.minimum(x, 0.0) - jnp.log(1.0 + jnp.exp(-jnp.abs(x)))


def _split3(x):
    hi = x.astype(BF16)
    r1 = x - hi.astype(F32)
    mid = r1.astype(BF16)
    lo = (r1 - mid.astype(F32)).astype(BF16)
    return hi, mid, lo


def _mlstm_kernel(qk_ref, v_ref, og_ref, gates_ref, gates_t_ref, gbias_ref, gbias_t_ref, cw_ref, cb_ref,
                  ghead_ref, tril_ref, triu_ref, o_ref, ext_ref, qk_act_ref, c_ref, n_ref, m_ref):
    L = MLSTM_L
    nqk = H_MLSTM * DK_MLSTM
    chunk = pl.program_id(1)

    @pl.when(chunk == 0)
    def _():
        c_ref[...] = jnp.zeros_like(c_ref)
        n_ref[...] = jnp.zeros_like(n_ref)
        m_ref[...] = jnp.zeros_like(m_ref)
        ext_ref[0:SUBLANES, :] = jnp.zeros((SUBLANES, 2 * nqk), F32)

    u = qk_ref[...].astype(F32)
    ext_ref[SUBLANES:SUBLANES + L, :] = u
    cw = cw_ref[...]
    y = cb_ref[...] + cw[CONV_WIDTH - 1:CONV_WIDTH, :] * u
    for back in range(1, CONV_WIDTH):
        w_row = cw[CONV_WIDTH - 1 - back:CONV_WIDTH - back, :]
        y = y + w_row * ext_ref[pl.ds(SUBLANES - back, L), :]
    ext_ref[0:SUBLANES, :] = u[L - SUBLANES:L, :]
    qk_act_ref[...] = y * jax.nn.sigmoid(y)

    g_col = gates_ref[...] + gbias_ref[...]
    g_row = gates_t_ref[...] + gbias_t_ref[...]
    lf_col = _log_sigmoid(g_col)
    lf_row = _log_sigmoid(g_row)
    b_col_all = sum(_dot(tril_ref[...], t) for t in _split3(lf_col))
    b_row_all = sum(_dot(t, triu_ref[...]) for t in _split3(lf_row))

    ti = lax.broadcasted_iota(jnp.int32, (L, L), 0)
    si = lax.broadcasted_iota(jnp.int32, (L, L), 1)
    causal = si <= ti

    for h in range(H_MLSTM):
        q = qk_act_ref[:, h * DK_MLSTM:(h + 1) * DK_MLSTM]
        k = qk_act_ref[:, nqk + h * DK_MLSTM:nqk + (h + 1) * DK_MLSTM] * (DK_MLSTM ** -0.5)
        v = v_ref[:, h * DV_MLSTM:(h + 1) * DV_MLSTM]
        qb = q.astype(BF16)
        kb = k.astype(BF16)
        li_col = g_col[:, h:h + 1]
        li_row = g_row[h:h + 1, :]
        b_col = b_col_all[:, H_MLSTM + h:H_MLSTM + h + 1]
        b_row = b_row_all[H_MLSTM + h:H_MLSTM + h + 1, :]
        c_st = c_ref[h]
        n_st = n_ref[h]
        m_st = m_ref[h]

        d_log = jnp.where(causal, b_col - b_row + li_row, NEG_INF)
        m_inter = b_col + m_st
        m_t = jnp.maximum(m_inter, jnp.max(d_log, axis=-1, keepdims=True))
        w_inter = jnp.exp(m_inter - m_t)
        s = _dot_nt(qb, kb) * jnp.exp(d_log - m_t)
        num = w_inter * _dot(qb, c_st.astype(BF16)) + _dot(s.astype(BF16), v)
        den = w_inter * jnp.sum(q * n_st, axis=-1, keepdims=True) + jnp.sum(s, axis=-1, keepdims=True)
        hh = num / jnp.maximum(jnp.abs(den), jnp.exp(-m_t))
        hh = hh * lax.rsqrt(jnp.mean(hh * hh, axis=-1, keepdims=True) + RMS_EPS)
        hh = hh * ghead_ref[:, h * DV_MLSTM:(h + 1) * DV_MLSTM]
        og = og_ref[:, h * DV_MLSTM:(h + 1) * DV_MLSTM].astype(F32)
        o_ref[:, h * DV_MLSTM:(h + 1) * DV_MLSTM] = (jax.nn.sigmoid(og) * hh).astype(o_ref.dtype)

        b_last = b_col[L - 1:L, :]
        g_dec_row = b_last - b_row + li_row
        g_dec_col = b_last - b_col + li_col
        m_new = jnp.maximum(b_last + m_st, jnp.max(g_dec_row, axis=-1, keepdims=True))
        w_old = jnp.exp(b_last + m_st - m_new)
        kw = k * jnp.exp(g_dec_col - m_new)
        c_ref[h] = w_old * c_st + _dot(kw.T.astype(BF16), v)
        n_ref[h] = w_old * n_st + jnp.sum(kw, axis=0, keepdims=True)
        m_ref[h] = m_new


def _mlstm(z3, gates3, gates_t, gbias, gbias_t, conv_w, conv_b, g_head, tril, triu, layer):
    b, t, _ = z3.shape
    L = MLSTM_L
    nc = t // L
    nqk = 2 * H_MLSTM * DK_MLSTM
    nv = H_MLSTM * DV_MLSTM
    qk_blk = (2 * D_ATT) // nqk
    v_blk = (2 * D_ATT + nqk) // nv
    og_blk = v_blk + 1
    return pl.pallas_call(
        _mlstm_kernel,
        grid=(b, nc),
        in_specs=[
            pl.BlockSpec((None, L, nqk), lambda i, c: (i, c, qk_blk)),
            pl.BlockSpec((None, L, nv), lambda i, c: (i, c, v_blk)),
            pl.BlockSpec((None, L, nv), lambda i, c: (i, c, og_blk)),
            pl.BlockSpec((None, L, LANES), lambda i, c: (i, c, 0)),
            pl.BlockSpec((SUBLANES, L), lambda i, c: (0, i * nc + c)),
            pl.BlockSpec((None, 1, LANES), lambda i, c: (layer, 0, 0)),
            pl.BlockSpec((None, SUBLANES, 1), lambda i, c: (layer, 0, 0)),
            pl.BlockSpec((None, CONV_WIDTH, nqk), lambda i, c: (layer, 0, 0)),
            pl.BlockSpec((None, 1, nqk), lambda i, c: (layer, 0, 0)),
            pl.BlockSpec((None, 1, nv), lambda i, c: (layer, 0, 0)),
            pl.BlockSpec((L, L), lambda i, c: (0, 0)),
            pl.BlockSpec((L, L), lambda i, c: (0, 0)),
        ],
        out_specs=pl.BlockSpec((None, L, nv), lambda i, c: (i, c, 0)),
        out_shape=jax.ShapeDtypeStruct((b, t, nv), BF16),
        scratch_shapes=[
            pltpu.VMEM((L + SUBLANES, nqk), F32),
            pltpu.VMEM((L, nqk), F32),
            pltpu.VMEM((H_MLSTM, DK_MLSTM, DV_MLSTM), F32),
            pltpu.VMEM((H_MLSTM, 1, DK_MLSTM), F32),
            pltpu.VMEM((H_MLSTM, 1, 1), F32),
        ],
        compiler_params=_cparams(("parallel", "arbitrary")),
        name="mlstm",
    )(z3, z3, z3, gates3, gates_t, gbias, gbias_t, conv_w, conv_b, g_head, tril, triu)


def _outproj_kernel(h_ref, a_ref, m_ref, c_ref, w_ref, o_ref):
    na = H_MOBA * HEAD_DIM
    nm = H_MLSTM * DV_MLSTM
    acc = _dot(a_ref[...], w_ref[0:na, :])
    acc += _dot(m_ref[...], w_ref[na:na + nm, :])
    acc += _dot(c_ref[...], w_ref[na + nm:, :])
    o_ref[...] = h_ref[...] + acc


def _outproj(h, out_a, out_m, out_c, w_out):
    m = h.shape[0]
    na, nm, ncol = out_a.shape[1], out_m.shape[1], out_c.shape[1]
    return pl.pallas_call(
        _outproj_kernel,
        grid=(m // TM,),
        in_specs=[
            pl.BlockSpec((TM, D_MODEL), lambda i: (i, 0)),
            pl.BlockSpec((TM, na), lambda i: (i, 0)),
            pl.BlockSpec((TM, nm), lambda i: (i, 0)),
            pl.BlockSpec((TM, ncol), lambda i: (i, 0)),
            pl.BlockSpec((na + nm + ncol, D_MODEL), lambda i: (0, 0)),
        ],
        out_specs=pl.BlockSpec((TM, D_MODEL), lambda i: (i, 0)),
        out_shape=jax.ShapeDtypeStruct((m, D_MODEL), F32),
        compiler_params=_cparams(("parallel",)),
        name="outproj",
    )(h, out_a, out_m, out_c, w_out)


def _ple_kernel(h_ref, g_ref, p_ref, wg_ref, wp_ref, gfin_ref, *refs, final):
    n_cast = len(refs) // 2
    o_ref = refs[n_cast]
    x = h_ref[...]
    xn = _rms(x, g_ref[...]).astype(BF16)
    gate = jax.nn.sigmoid(_dot(xn, wg_ref[...]))
    proj = _dot(p_ref[...].astype(BF16), wp_ref[...])
    y = x + gate * proj
    if final:
        y = _rms(y, gfin_ref[...])
    o_ref[...] = y
    _cast_blocks(refs[:n_cast], refs[n_cast + 1:])


def _ple(h, g, p2, w_gate, w_proj, g_final, layer, next_w_in_t=None):
    m = h.shape[0]
    nm = m // TM
    in_specs = [
        pl.BlockSpec((TM, D_MODEL), lambda i: (i, 0)),
        pl.BlockSpec((None, 1, D_MODEL), lambda i: (layer, 0, 0)),
        pl.BlockSpec((None, TM, D_PLE), lambda i: (layer, i, 0)),
        pl.BlockSpec((D_MODEL, D_MODEL), lambda i: (0, 0)),
        pl.BlockSpec((D_PLE, D_MODEL), lambda i: (0, 0)),
        pl.BlockSpec((1, D_MODEL), lambda i: (0, 0)),
    ]
    out_specs = [pl.BlockSpec((TM, D_MODEL), lambda i: (i, 0))]
    out_shape = [jax.ShapeDtypeStruct((m, D_MODEL), F32)]
    inputs = [h, g, p2, w_gate, w_proj, g_final]
    if next_w_in_t is not None:
        src, nl = next_w_in_t
        blk = (D_Z // nm, D_MODEL)
        in_specs.append(pl.BlockSpec((None, *blk), lambda i: (nl, i, 0)))
        out_specs.append(pl.BlockSpec(blk, lambda i: (i, 0)))
        out_shape.append(jax.ShapeDtypeStruct((D_Z, D_MODEL), BF16))
        inputs.append(src)
    outs = pl.pallas_call(
        functools.partial(_ple_kernel, final=next_w_in_t is None),
        grid=(nm,),
        in_specs=in_specs,
        out_specs=out_specs,
        out_shape=out_shape,
        compiler_params=_cparams(("parallel",)),
        name="ple",
    )(*inputs)
    return outs[0], (outs[1] if len(outs) > 1 else None)


def kernel(x, p, g_ffn1, w_up1, w_down1, g_mix, w_in, conv_w, conv_b, b_igate, b_fgate, g_head, w_out, g_ffn2, w_up2, w_down2, g_ple, w_ple_gate, w_ple_proj, g_final):
    b, t, d = x.shape
    depth = p.shape[0]
    m = b * t

    ffn_w = (w_up1[0].astype(BF16), w_down1[0].astype(BF16))
    w_in_t = jnp.swapaxes(w_in, 1, 2)
    w_in_tb = w_in_t[0, :D_Z].astype(BF16)
    w_gate_row = w_in_t[:, D_Z:, :].astype(BF16)
    w_gate_col = jnp.pad(w_in[:, :, D_Z:], ((0, 0), (0, 0), (0, LANES - N_GATES))).astype(BF16)
    gate_bias = jnp.concatenate([b_igate, b_fgate], axis=-1)
    gbias_col = jnp.pad(gate_bias, ((0, 0), (0, LANES - N_GATES)))[:, None, :]
    gbias_row = gate_bias[:, :, None]
    row = lambda a: a[:, None, :]
    g_ffn1r, g_mixr, g_ffn2r, g_pler = row(g_ffn1), row(g_mix), row(g_ffn2), row(g_ple)
    conv_br, g_headr = row(conv_b), row(g_head)
    g_finalr = g_final[None, :]
    p2 = p.reshape(depth, m, D_PLE)

    slopes = 2.0 ** (-8.0 * jnp.arange(1, N_ALIBI + 1, dtype=F32) / N_ALIBI)
    slopes_moba, slopes_dil = slopes[0::2], slopes[1::2]
    causal_mask = jnp.asarray(_causal_mask(t))
    log_counts = jnp.asarray(_dilated_window_log_counts(t))
    tri = np.tril(np.ones((MLSTM_L, MLSTM_L), np.float32))
    tril, triu = jnp.asarray(tri, BF16), jnp.asarray(tri.T, BF16)

    h = x.reshape(m, d)
    for i in range(depth):
        h, ffn_w = _ffn(h, g_ffn1r, *ffn_w, i, next_weights=(w_up2, w_down2, i))
        z, gates, gates_t, (w_outb, w_pgb, w_ppb) = _inproj(
            h, g_mixr, w_in_tb, w_gate_col, w_gate_row, i, (w_out, w_ple_gate, w_ple_proj))
        z3 = z.reshape(b, t, D_Z)
        out_a = _attention(z3, slopes_moba, causal_mask, 0, MOBA_TOPK, "moba")
        out_c = _attention(z3, slopes_dil, log_counts, D_ATT, None, "dilated")
        out_m = _mlstm(z3, gates.reshape(b, t, LANES), gates_t, gbias_col, gbias_row,
                       conv_w, conv_br, g_headr, tril, triu, i)
        h = _outproj(h, out_a.reshape(m, -1), out_m.reshape(m, -1), out_c.reshape(m, -1), w_outb)
        upcoming = (w_up1, w_down1, i + 1) if i + 1 < depth else None
        h, ffn_w = _ffn(h, g_ffn2r, *ffn_w, i, next_weights=upcoming)
        h, w_in_tb = _ple(h, g_pler, p2, w_pgb, w_ppb, g_finalr, i,
                          next_w_in_t=(w_in_t, i + 1) if i + 1 < depth else None)
    return h.reshape(b, t, d)
```

```python
import functools

import numpy as np
import jax
import jax.numpy as jnp
from jax import lax
from jax.experimental import pallas as pl
from jax.experimental.pallas import tpu as pltpu

F32 = jnp.float32
BF16 = jnp.bfloat16

D_MODEL = 2048
DEPTH = 4
HEAD_DIM = 128
H_MOBA = 4
H_MLSTM = 4
H_DIL = 4
DK_MLSTM = 128
DV_MLSTM = 256
D_FF = 5632
D_PLE = 256
MOBA_BLOCK = 256
MOBA_TOPK = 3
DIL_PAIRS = ((128, 1), (512, 4), (2048, 16))
CONV_WIDTH = 4
RMS_EPS = 1e-6
LOG2_E = 1.4426950408889634
NEG_INF = -1e30
N_ALIBI = H_MOBA + H_DIL
D_ATT = 3 * H_MOBA * HEAD_DIM
D_Z = 2 * D_ATT + 2 * H_MLSTM * DK_MLSTM + 2 * H_MLSTM * DV_MLSTM
N_GATES = 2 * H_MLSTM

LANES = 128
SUBLANES = 8
VMEM_LIMIT_BYTES = 60 * 1024 * 1024

TM = 512
TM_BIG = 1024
TF = 512
TN_IN = 1536
ATT_BLOCK = 256
MLSTM_L = 256


def _cparams(semantics):
    return pltpu.CompilerParams(dimension_semantics=semantics, vmem_limit_bytes=VMEM_LIMIT_BYTES)


def _rms(x, g):
    ms = jnp.mean(x * x, axis=-1, keepdims=True)
    return x * lax.rsqrt(ms + RMS_EPS) * g


def _dot(a, b):
    return jnp.dot(a, b, preferred_element_type=F32)


def _dot_nt(a, b):
    return lax.dot_general(a, b, (((1,), (1,)), ((), ())), preferred_element_type=F32)


def _cast_blocks(src_refs, dst_refs):
    for src, dst in zip(src_refs, dst_refs):
        dst[...] = src[...].astype(BF16)


def _ffn_kernel(*refs, n_cast):
    x_ref, g_ref, wg_ref, wu_ref, wd_ref = refs[:5]
    o_ref = refs[5 + n_cast]
    xn_ref = refs[-1]
    f = pl.program_id(1)

    @pl.when(f == 0)
    def _():
        xn_ref[...] = _rms(x_ref[...], g_ref[...]).astype(BF16)
        o_ref[...] = jnp.zeros_like(o_ref)

    xn = xn_ref[...]
    gate = _dot(xn, wg_ref[...])
    up = _dot(xn, wu_ref[...])
    act = (gate * jax.nn.sigmoid(gate) * up).astype(BF16)
    o_ref[...] += _dot(act, wd_ref[...])
    _cast_blocks(refs[5:5 + n_cast], refs[6 + n_cast:6 + 2 * n_cast])

    @pl.when(f == pl.num_programs(1) - 1)
    def _():
        o_ref[...] = x_ref[...] + 0.5 * o_ref[...]


def _ffn(h, g, w_up, w_down, layer, next_weights=None):
    m = h.shape[0]
    nm, nf = m // TM_BIG, D_FF // TF
    in_specs = [
        pl.BlockSpec((TM_BIG, D_MODEL), lambda i, f: (i, 0)),
        pl.BlockSpec((None, 1, D_MODEL), lambda i, f: (layer, 0, 0)),
        pl.BlockSpec((D_MODEL, TF), lambda i, f: (0, f)),
        pl.BlockSpec((D_MODEL, TF), lambda i, f: (0, f + nf)),
        pl.BlockSpec((TF, D_MODEL), lambda i, f: (f, 0)),
    ]
    out_specs = [pl.BlockSpec((TM_BIG, D_MODEL), lambda i, f: (i, 0))]
    out_shape = [jax.ShapeDtypeStruct((m, D_MODEL), F32)]
    inputs = [h, g, w_up, w_up, w_down]
    if next_weights is not None:
        src_up, src_down, nl = next_weights
        up_blk = (D_MODEL // nm, 2 * D_FF // nf)
        down_blk = (D_FF // nf, D_MODEL // nm)
        in_specs += [
            pl.BlockSpec((None, *up_blk), lambda i, f: (nl, i, f)),
            pl.BlockSpec((None, *down_blk), lambda i, f: (nl, f, i)),
        ]
        out_specs += [pl.BlockSpec(up_blk, lambda i, f: (i, f)), pl.BlockSpec(down_blk, lambda i, f: (f, i))]
        out_shape += [jax.ShapeDtypeStruct(src_up.shape[1:], BF16), jax.ShapeDtypeStruct(src_down.shape[1:], BF16)]
        inputs += [src_up, src_down]
    outs = pl.pallas_call(
        functools.partial(_ffn_kernel, n_cast=len(inputs) - 5),
        grid=(nm, nf),
        in_specs=in_specs,
        out_specs=out_specs,
        out_shape=out_shape,
        scratch_shapes=[pltpu.VMEM((TM_BIG, D_MODEL), BF16)],
        compiler_params=_cparams(("parallel", "arbitrary")),
        name="ffn",
    )(*inputs)
    return outs[0], tuple(outs[1:])


def _inproj_kernel(x_ref, g_ref, w_ref, wgate_t_ref, *refs):
    n_cast = (len(refs) - 3) // 2
    z_ref, gates_t_ref = refs[n_cast:n_cast + 2]
    xn_ref = refs[-1]
    n = pl.program_id(1)

    @pl.when(n == 0)
    def _():
        xn = _rms(x_ref[...], g_ref[...]).astype(BF16)
        xn_ref[...] = xn
        gates_t_ref[...] = _dot_nt(wgate_t_ref[...].astype(BF16), xn)

    z_ref[...] = _dot_nt(xn_ref[...], w_ref[...]).astype(z_ref.dtype)
    _cast_blocks(refs[:n_cast], refs[n_cast + 2:2 * n_cast + 2])


def _inproj(h, g, w_in_t, w_gate_t, layer, later_weights):
    m = h.shape[0]
    nm, nn = m // TM_BIG, D_Z // TN_IN
    in_specs = [
        pl.BlockSpec((TM_BIG, D_MODEL), lambda i, n: (i, 0)),
        pl.BlockSpec((None, 1, D_MODEL), lambda i, n: (layer, 0, 0)),
        pl.BlockSpec((TN_IN, D_MODEL), lambda i, n: (n, 0)),
        pl.BlockSpec((None, LANES, D_MODEL), lambda i, n: (layer, 0, 0)),
    ]
    out_specs = [
        pl.BlockSpec((TM_BIG, TN_IN), lambda i, n: (i, n)),
        pl.BlockSpec((LANES, TM_BIG), lambda i, n: (0, i)),
    ]
    out_shape = [
        jax.ShapeDtypeStruct((m, D_Z), BF16),
        jax.ShapeDtypeStruct((LANES, m), F32),
    ]
    for w in later_weights:
        blk = (w.shape[1] // nm, w.shape[2] // nn)
        in_specs.append(pl.BlockSpec((None, *blk), lambda i, n: (layer, i, n)))
        out_specs.append(pl.BlockSpec(blk, lambda i, n: (i, n)))
        out_shape.append(jax.ShapeDtypeStruct(w.shape[1:], BF16))
    outs = pl.pallas_call(
        _inproj_kernel,
        grid=(nm, nn),
        in_specs=in_specs,
        out_specs=out_specs,
        out_shape=out_shape,
        scratch_shapes=[pltpu.VMEM((TM_BIG, D_MODEL), BF16)],
        compiler_params=_cparams(("parallel", "arbitrary")),
        name="inproj",
    )(h, g, w_in_t, w_gate_t, *later_weights)
    return outs[0], outs[1], tuple(outs[2:])


def _head_cols(h):
    return slice(h * HEAD_DIM, (h + 1) * HEAD_DIM)


def _stage_values(v_ref, vt_ref, nblk):
    for h in range(vt_ref.shape[0]):
        for j in range(nblk):
            vj = v_ref[j * ATT_BLOCK:(j + 1) * ATT_BLOCK, _head_cols(h)].astype(F32)
            vt_ref[h, :, j * ATT_BLOCK:(j + 1) * ATT_BLOCK] = vj.T.astype(BF16)


def _local_distance():
    qi = lax.broadcasted_iota(jnp.int32, (ATT_BLOCK, ATT_BLOCK), 1)
    ki = lax.broadcasted_iota(jnp.int32, (ATT_BLOCK, ATT_BLOCK), 0)
    return (qi - ki).astype(F32)


def _fold_sublanes(x, op):
    return op(x.reshape(ATT_BLOCK // SUBLANES, SUBLANES, x.shape[-1]), axis=0)


def _softmax_pv(cc, score_fns, s_ref, p_ref, vt_ref, o_ref):
    nk = (cc + 1) * ATT_BLOCK
    blocks = [slice(j * ATT_BLOCK, (j + 1) * ATT_BLOCK) for j in range(cc + 1)]
    maxima = []
    for h, score_fn in enumerate(score_fns):
        m8 = None
        for j, blk in enumerate(blocks):
            s = score_fn(j)
            s_ref[h, blk, :] = s
            part = _fold_sublanes(s, jnp.max)
            m8 = part if m8 is None else jnp.maximum(m8, part)
        maxima.append(jnp.max(m8, axis=0, keepdims=True))
    for h, m in enumerate(maxima):
        l8 = None
        for blk in blocks:
            p = jnp.exp2(s_ref[h, blk, :] - m)
            p_ref[h, blk, :] = p.astype(BF16)
            part = _fold_sublanes(p, jnp.sum)
            l8 = part if l8 is None else l8 + part
        l = jnp.sum(l8, axis=0, keepdims=True)
        acc = _dot(vt_ref[h, :, 0:nk], p_ref[h, 0:nk, :])
        o_ref[:, h * HEAD_DIM:(h + 1) * HEAD_DIM] = (acc / l).T.astype(o_ref.dtype)


def _attn_kernel(slope_ref, q_ref, k_ref, v_ref, mask_ref, o_ref, vt_ref, s_ref, p_ref, bias_ref, kmean_ref, *,
                 nblk, topk):
    c = pl.program_id(1)
    nh = vt_ref.shape[0]
    qk_scale = HEAD_DIM ** -0.5 * LOG2_E

    @pl.when(c == 0)
    def _():
        _stage_values(v_ref, vt_ref, nblk)
        dist = _local_distance()
        for h in range(nh):
            slope2 = slope_ref[h] * LOG2_E
            for diff in range(nblk):
                bias_ref[h, diff] = mask_ref[diff] - slope2 * (dist + float(diff * ATT_BLOCK))
            if topk is not None:
                for j in range(nblk):
                    kj = k_ref[j * ATT_BLOCK:(j + 1) * ATT_BLOCK, _head_cols(h)].astype(F32)
                    kmean_ref[h, j:j + 1, :] = jnp.mean(kj, axis=0, keepdims=True)

    def head_scores(cc, h):
        q = q_ref[:, _head_cols(h)]
        if topk is not None and cc > topk:
            km = kmean_ref[h]
            km_hi = km.astype(BF16)
            km_lo = (km - km_hi.astype(F32)).astype(BF16)
            gate = _dot_nt(km_hi, q) + _dot_nt(km_lo, q)
            g = [gate[n:n + 1, :] for n in range(cc)]
            sel = []
            for n in range(cc):
                n_ahead = jnp.zeros_like(g[n])
                for mth in range(cc):
                    if mth != n:
                        ahead = (g[mth] >= g[n]) if mth < n else (g[mth] > g[n])
                        n_ahead = n_ahead + jnp.where(ahead, 1.0, 0.0)
                sel.append(jnp.where(n_ahead < topk, 0.0, NEG_INF))
        else:
            sel = None

        def score(j):
            kj = k_ref[j * ATT_BLOCK:(j + 1) * ATT_BLOCK, _head_cols(h)]
            s = _dot_nt(kj, q) * qk_scale + bias_ref[h, cc - j]
            return s if (sel is None or j == cc) else s + sel[j]

        return score

    def branch(cc):
        _softmax_pv(cc, [head_scores(cc, h) for h in range(nh)], s_ref, p_ref, vt_ref, o_ref)

    for cc in range(nblk):
        pl.when(c == cc)(functools.partial(branch, cc))


def _block_distances(seq):
    nblk = seq // ATT_BLOCK
    kl = np.arange(ATT_BLOCK)[None, :, None]
    ql = np.arange(ATT_BLOCK)[None, None, :]
    return np.arange(nblk)[:, None, None] * ATT_BLOCK + ql - kl


def _causal_mask(seq):
    return np.where(_block_distances(seq) >= 0, 0.0, NEG_INF).astype(np.float32)


def _dilated_window_log_counts(seq):
    d = _block_distances(seq)
    cnt = np.zeros(d.shape, np.float64)
    for window, dil in DIL_PAIRS:
        cnt += (d >= 0) & (d <= window) & (d % dil == 0)
    return np.where(cnt > 0, np.log2(np.maximum(cnt, 1.0)), NEG_INF).astype(np.float32)


def _attention(z3, slopes, mask, col0, topk, name):
    b, t, _ = z3.shape
    nblk = t // ATT_BLOCK
    nh = slopes.shape[0]
    width = nh * HEAD_DIM
    cb = col0 // width
    return pl.pallas_call(
        functools.partial(_attn_kernel, nblk=nblk, topk=topk),
        grid=(b, nblk),
        in_specs=[
            pl.BlockSpec(memory_space=pltpu.SMEM),
            pl.BlockSpec((None, ATT_BLOCK, width), lambda i, c: (i, c, cb)),
            pl.BlockSpec((None, t, width), lambda i, c: (i, 0, cb + 1)),
            pl.BlockSpec((None, t, width), lambda i, c: (i, 0, cb + 2)),
            pl.BlockSpec((nblk, ATT_BLOCK, ATT_BLOCK), lambda i, c: (0, 0, 0)),
        ],
        out_specs=pl.BlockSpec((None, ATT_BLOCK, width), lambda i, c: (i, c, 0)),
        out_shape=jax.ShapeDtypeStruct((b, t, width), BF16),
        scratch_shapes=[
            pltpu.VMEM((nh, HEAD_DIM, t), BF16),
            pltpu.VMEM((nh, t, ATT_BLOCK), F32),
            pltpu.VMEM((nh, t, ATT_BLOCK), BF16),
            pltpu.VMEM((nh, nblk, ATT_BLOCK, ATT_BLOCK), F32),
            pltpu.VMEM((nh, nblk, HEAD_DIM), F32),
        ],
        compiler_params=_cparams(("parallel", "arbitrary")),
        name=name,
    )(slopes, z3, z3, z3, mask)


def _log_sigmoid(x):
    return jnp.minimum(x, 0.0) - jnp.log(1.0 + jnp.exp(-jnp.abs(x)))


def _split3(x):
    hi = x.astype(BF16)
    r1 = x - hi.astype(F32)
    mid = r1.astype(BF16)
    lo = (r1 - mid.astype(F32)).astype(BF16)
    return hi, mid, lo


def _mlstm_kernel(qk_ref, v_ref, og_ref, gates_t_ref, gbias_ref, gbias_t_ref, cw_ref, cb_ref,
                  ghead_ref, tril_ref, triu_ref, shift_ref, o_ref, ext_ref, qk_act_ref, c_ref, n_ref, m_ref):
    L = MLSTM_L
    nqk = H_MLSTM * DK_MLSTM
    chunk = pl.program_id(1)

    @pl.when(chunk == 0)
    def _():
        c_ref[...] = jnp.zeros_like(c_ref)
        n_ref[...] = jnp.zeros_like(n_ref)
        m_ref[...] = jnp.zeros_like(m_ref)
        ext_ref[0:SUBLANES, :] = jnp.zeros((SUBLANES, 2 * nqk), F32)

    u_bf = qk_ref[...]
    u = u_bf.astype(F32)
    shifted = _dot(shift_ref[...], u_bf)
    cw = cw_ref[...]
    w_rows = [cw[CONV_WIDTH - 1 - back:CONV_WIDTH - back, :] for back in range(CONV_WIDTH)]
    y = cb_ref[...] + w_rows[0] * u
    ext_ref[SUBLANES:2 * SUBLANES, :] = u[0:SUBLANES, :]
    y_head = cb_ref[...] + w_rows[0] * u[0:SUBLANES, :]
    for back in range(1, CONV_WIDTH):
        y = y + w_rows[back] * shifted[(back - 1) * L:back * L, :]
        y_head = y_head + w_rows[back] * ext_ref[SUBLANES - back:2 * SUBLANES - back, :]
    ext_ref[0:SUBLANES, :] = u[L - SUBLANES:L, :]
    qk_act_ref[...] = y * jax.nn.sigmoid(y)
    qk_act_ref[0:SUBLANES, :] = y_head * jax.nn.sigmoid(y_head)

    gates_t = gates_t_ref[...]
    g_row = gates_t[0:N_GATES, :] + gbias_t_ref[...]
    g_col = gates_t.T + gbias_ref[...]
    lf_col = _log_sigmoid(g_col)
    lf_row = _log_sigmoid(g_row)
    b_col_all = sum(_dot(tril_ref[...], t) for t in _split3(lf_col))
    b_row_all = sum(_dot(t, triu_ref[...]) for t in _split3(lf_row))

    ti = lax.broadcasted_iota(jnp.int32, (L, L), 0)
    si = lax.broadcasted_iota(jnp.int32, (L, L), 1)
    causal = si <= ti

    heads = []
    for h in range(H_MLSTM):
        q = qk_act_ref[:, h * DK_MLSTM:(h + 1) * DK_MLSTM]
        k = qk_act_ref[:, nqk + h * DK_MLSTM:nqk + (h + 1) * DK_MLSTM] * (DK_MLSTM ** -0.5)
        qb = q.astype(BF16)
        li_col = g_col[:, h:h + 1]
        li_row = g_row[h:h + 1, :]
        b_col = b_col_all[:, H_MLSTM + h:H_MLSTM + h + 1]
        b_row = b_row_all[H_MLSTM + h:H_MLSTM + h + 1, :]
        m_st = m_ref[h]
        d_log = jnp.where(causal, b_col - b_row + li_row, NEG_INF)
        m_inter = b_col + m_st
        m_t = jnp.maximum(m_inter, jnp.max(d_log, axis=-1, keepdims=True))
        s = _dot_nt(qb, k.astype(BF16)) * jnp.exp(d_log - m_t)
        heads.append(dict(q=q, k=k, qb=qb, s=s, m_t=m_t, w_inter=jnp.exp(m_inter - m_t), m_st=m_st,
                          c_st=c_ref[h], n_st=n_ref[h],
                          li_col=li_col, li_row=li_row, b_col=b_col, b_row=b_row))

    for h, hd in enumerate(heads):
        v = v_ref[:, h * DV_MLSTM:(h + 1) * DV_MLSTM]
        b_last = hd["b_col"][L - 1:L, :]
        g_dec_row = b_last - hd["b_row"] + hd["li_row"]
        g_dec_col = b_last - hd["b_col"] + hd["li_col"]
        m_new = jnp.maximum(b_last + hd["m_st"], jnp.max(g_dec_row, axis=-1, keepdims=True))
        w_old = jnp.exp(b_last + hd["m_st"] - m_new)
        kw = hd["k"] * jnp.exp(g_dec_col - m_new)
        c_ref[h] = w_old * hd["c_st"] + _dot(kw.T.astype(BF16), v)
        n_ref[h] = w_old * hd["n_st"] + jnp.sum(kw, axis=0, keepdims=True)
        m_ref[h] = m_new

    for h, hd in enumerate(heads):
        cols = slice(h * DV_MLSTM, (h + 1) * DV_MLSTM)
        v = v_ref[:, cols]
        num = hd["w_inter"] * _dot(hd["qb"], hd["c_st"].astype(BF16)) + _dot(hd["s"].astype(BF16), v)
        den = (hd["w_inter"] * jnp.sum(hd["q"] * hd["n_st"], axis=-1, keepdims=True)
               + jnp.sum(hd["s"], axis=-1, keepdims=True))
        hh = num / jnp.maximum(jnp.abs(den), jnp.exp(-hd["m_t"]))
        hh = hh * lax.rsqrt(jnp.mean(hh * hh, axis=-1, keepdims=True) + RMS_EPS)
        hh = hh * ghead_ref[:, cols]
        o_ref[:, cols] = (jax.nn.sigmoid(og_ref[:, cols].astype(F32)) * hh).astype(o_ref.dtype)


def _conv_shift_matrices(length):
    t = np.arange(length)
    mats = [(t[:, None] - t[None, :] == s).astype(np.float32) for s in range(1, CONV_WIDTH)]
    return np.concatenate(mats, axis=0)


def _mlstm(z3, gates_t, gbias, gbias_t, conv_w, conv_b, g_head, tril, triu, shifts, layer):
    b, t, _ = z3.shape
    L = MLSTM_L
    nc = t // L
    nqk = 2 * H_MLSTM * DK_MLSTM
    nv = H_MLSTM * DV_MLSTM
    qk_blk = (2 * D_ATT) // nqk
    v_blk = (2 * D_ATT + nqk) // nv
    og_blk = v_blk + 1
    return pl.pallas_call(
        _mlstm_kernel,
        grid=(b, nc),
        in_specs=[
            pl.BlockSpec((None, L, nqk), lambda i, c: (i, c, qk_blk)),
            pl.BlockSpec((None, L, nv), lambda i, c: (i, c, v_blk)),
            pl.BlockSpec((None, L, nv), lambda i, c: (i, c, og_blk)),
            pl.BlockSpec((LANES, L), lambda i, c: (0, i * nc + c)),
            pl.BlockSpec((None, 1, LANES), lambda i, c: (layer, 0, 0)),
            pl.BlockSpec((None, N_GATES, 1), lambda i, c: (layer, 0, 0)),
            pl.BlockSpec((None, CONV_WIDTH, nqk), lambda i, c: (layer, 0, 0)),
            pl.BlockSpec((None, 1, nqk), lambda i, c: (layer, 0, 0)),
            pl.BlockSpec((None, 1, nv), lambda i, c: (layer, 0, 0)),
            pl.BlockSpec((L, L), lambda i, c: (0, 0)),
            pl.BlockSpec((L, L), lambda i, c: (0, 0)),
            pl.BlockSpec(((CONV_WIDTH - 1) * L, L), lambda i, c: (0, 0)),
        ],
        out_specs=pl.BlockSpec((None, L, nv), lambda i, c: (i, c, 0)),
        out_shape=jax.ShapeDtypeStruct((b, t, nv), BF16),
        scratch_shapes=[
            pltpu.VMEM((2 * SUBLANES, nqk), F32),
            pltpu.VMEM((L, nqk), F32),
            pltpu.VMEM((H_MLSTM, DK_MLSTM, DV_MLSTM), F32),
            pltpu.VMEM((H_MLSTM, 1, DK_MLSTM), F32),
            pltpu.VMEM((H_MLSTM, 1, 1), F32),
        ],
        compiler_params=_cparams(("parallel", "arbitrary")),
        name="mlstm",
    )(z3, z3, z3, gates_t, gbias, gbias_t, conv_w, conv_b, g_head, tril, triu, shifts)


def _outproj_kernel(h_ref, a_ref, m_ref, c_ref, w_ref, o_ref):
    na = H_MOBA * HEAD_DIM
    nm = H_MLSTM * DV_MLSTM
    acc = _dot(a_ref[...], w_ref[0:na, :])
    acc += _dot(m_ref[...], w_ref[na:na + nm, :])
    acc += _dot(c_ref[...], w_ref[na + nm:, :])
    o_ref[...] = h_ref[...] + acc


def _outproj(h, out_a, out_m, out_c, w_out):
    m = h.shape[0]
    na, nm, ncol = out_a.shape[1], out_m.shape[1], out_c.shape[1]
    return pl.pallas_call(
        _outproj_kernel,
        grid=(m // TM,),
        in_specs=[
            pl.BlockSpec((TM, D_MODEL), lambda i: (i, 0)),
            pl.BlockSpec((TM, na), lambda i: (i, 0)),
            pl.BlockSpec((TM, nm), lambda i: (i, 0)),
            pl.BlockSpec((TM, ncol), lambda i: (i, 0)),
            pl.BlockSpec((na + nm + ncol, D_MODEL), lambda i: (0, 0)),
        ],
        out_specs=pl.BlockSpec((TM, D_MODEL), lambda i: (i, 0)),
        out_shape=jax.ShapeDtypeStruct((m, D_MODEL), F32),
        compiler_params=_cparams(("parallel",)),
        name="outproj",
    )(h, out_a, out_m, out_c, w_out)


def _ple_kernel(h_ref, g_ref, p_ref, wg_ref, wp_ref, gfin_ref, *refs, final):
    n_cast = len(refs) // 2
    o_ref = refs[n_cast]
    x = h_ref[...]
    xn = _rms(x, g_ref[...]).astype(BF16)
    gate = jax.nn.sigmoid(_dot(xn, wg_ref[...]))
    proj = _dot(p_ref[...].astype(BF16), wp_ref[...])
    y = x + gate * proj
    if final:
        y = _rms(y, gfin_ref[...])
    o_ref[...] = y
    _cast_blocks(refs[:n_cast], refs[n_cast + 1:])


def _ple(h, g, p2, w_gate, w_proj, g_final, layer, next_w_in_t=None):
    m = h.shape[0]
    nm = m // TM
    in_specs = [
        pl.BlockSpec((TM, D_MODEL), lambda i: (i, 0)),
        pl.BlockSpec((None, 1, D_MODEL), lambda i: (layer, 0, 0)),
        pl.BlockSpec((None, TM, D_PLE), lambda i: (layer, i, 0)),
        pl.BlockSpec((D_MODEL, D_MODEL), lambda i: (0, 0)),
        pl.BlockSpec((D_PLE, D_MODEL), lambda i: (0, 0)),
        pl.BlockSpec((1, D_MODEL), lambda i: (0, 0)),
    ]
    out_specs = [pl.BlockSpec((TM, D_MODEL), lambda i: (i, 0))]
    out_shape = [jax.ShapeDtypeStruct((m, D_MODEL), F32)]
    inputs = [h, g, p2, w_gate, w_proj, g_final]
    if next_w_in_t is not None:
        src, nl = next_w_in_t
        blk = (D_Z // nm, D_MODEL)
        in_specs.append(pl.BlockSpec((None, *blk), lambda i: (nl, i, 0)))
        out_specs.append(pl.BlockSpec(blk, lambda i: (i, 0)))
        out_shape.append(jax.ShapeDtypeStruct((D_Z, D_MODEL), BF16))
        inputs.append(src)
    outs = pl.pallas_call(
        functools.partial(_ple_kernel, final=next_w_in_t is None),
        grid=(nm,),
        in_specs=in_specs,
        out_specs=out_specs,
        out_shape=out_shape,
        compiler_params=_cparams(("parallel",)),
        name="ple",
    )(*inputs)
    return outs[0], (outs[1] if len(outs) > 1 else None)


def _cast_kernel(src_ref, dst_ref):
    dst_ref[...] = src_ref[...].astype(BF16)


def _cast_rows(src, layer, rows):
    cols = src.shape[2]
    return pl.pallas_call(
        _cast_kernel,
        grid=(rows // TM,),
        in_specs=[pl.BlockSpec((None, TM, cols), lambda i: (layer, i, 0))],
        out_specs=pl.BlockSpec((TM, cols), lambda i: (i, 0)),
        out_shape=jax.ShapeDtypeStruct((rows, cols), BF16),
        compiler_params=_cparams(("parallel",)),
        name="cast_rows",
    )(src)


def kernel(x, p, g_ffn1, w_up1, w_down1, g_mix, w_in, conv_w, conv_b, b_igate, b_fgate, g_head, w_out, g_ffn2, w_up2, w_down2, g_ple, w_ple_gate, w_ple_proj, g_final):
    b, t, d = x.shape
    depth = p.shape[0]
    m = b * t

    ffn_w = (w_up1[0].astype(BF16), w_down1[0].astype(BF16))
    w_in_t = jnp.swapaxes(w_in, 1, 2)
    w_in_tb = _cast_rows(w_in_t, 0, D_Z)
    w_gate_t = jnp.pad(w_in_t[:, D_Z:, :], ((0, 0), (0, LANES - N_GATES), (0, 0)))
    gate_bias = jnp.concatenate([b_igate, b_fgate], axis=-1)
    gbias_col = jnp.pad(gate_bias, ((0, 0), (0, LANES - N_GATES)))[:, None, :]
    gbias_row = gate_bias[:, :, None]
    row = lambda a: a[:, None, :]
    g_ffn1r, g_mixr, g_ffn2r, g_pler = row(g_ffn1), row(g_mix), row(g_ffn2), row(g_ple)
    conv_br, g_headr = row(conv_b), row(g_head)
    g_finalr = g_final[None, :]
    p2 = p.reshape(depth, m, D_PLE)

    slopes = 2.0 ** (-8.0 * jnp.arange(1, N_ALIBI + 1, dtype=F32) / N_ALIBI)
    slopes_moba, slopes_dil = slopes[0::2], slopes[1::2]
    causal_mask = jnp.asarray(_causal_mask(t))
    log_counts = jnp.asarray(_dilated_window_log_counts(t))
    tri = np.tril(np.ones((MLSTM_L, MLSTM_L), np.float32))
    tril, triu = jnp.asarray(tri, BF16), jnp.asarray(tri.T, BF16)
    shifts = jnp.asarray(_conv_shift_matrices(MLSTM_L), BF16)

    h = x.reshape(m, d)
    for i in range(depth):
        h, ffn_w = _ffn(h, g_ffn1r, *ffn_w, i, next_weights=(w_up2, w_down2, i))
        z, gates_t, (w_outb, w_pgb, w_ppb) = _inproj(
            h, g_mixr, w_in_tb, w_gate_t, i, (w_out, w_ple_gate, w_ple_proj))
        z3 = z.reshape(b, t, D_Z)
        out_a = _attention(z3, slopes_moba, causal_mask, 0, MOBA_TOPK, "moba")
        out_c = _attention(z3, slopes_dil, log_counts, D_ATT, None, "dilated")
        out_m = _mlstm(z3, gates_t, gbias_col, gbias_row, conv_w, conv_br, g_headr, tril, triu, shifts, i)
        h = _outproj(h, out_a.reshape(m, -1), out_m.reshape(m, -1), out_c.reshape(m, -1), w_outb)
        upcoming = (w_up1, w_down1, i + 1) if i + 1 < depth else None
        h, ffn_w = _ffn(h, g_ffn2r, *ffn_w, i, next_weights=upcoming)
        h, w_in_tb = _ple(h, g_pler, p2, w_pgb, w_ppb, g_finalr, i,
                          next_w_in_t=(w_in_t, i + 1) if i + 1 < depth else None)
    return h.reshape(b, t, d)
```

```python
import functools

import numpy as np
import jax
import jax.numpy as jnp
from jax import lax
from jax.experimental import pallas as pl
from jax.experimental.pallas import tpu as pltpu

F32 = jnp.float32
BF16 = jnp.bfloat16

D_MODEL = 2048
DEPTH = 4
HEAD_DIM = 128
H_MOBA = 4
H_MLSTM = 4
H_DIL = 4
DK_MLSTM = 128
DV_MLSTM = 256
D_FF = 5632
D_PLE = 256
MOBA_BLOCK = 256
MOBA_TOPK = 3
DIL_PAIRS = ((128, 1), (512, 4), (2048, 16))
CONV_WIDTH = 4
RMS_EPS = 1e-6
LOG2_E = 1.4426950408889634
NEG_INF = -1e30
N_ALIBI = H_MOBA + H_DIL
D_ATT = 3 * H_MOBA * HEAD_DIM
D_Z = 2 * D_ATT + 2 * H_MLSTM * DK_MLSTM + 2 * H_MLSTM * DV_MLSTM
N_GATES = 2 * H_MLSTM

LANES = 128
SUBLANES = 8
VMEM_LIMIT_BYTES = 60 * 1024 * 1024

TM = 512
TM_BIG = 1024
TF = 512
TN_IN = 1536
ATT_BLOCK = 256
MLSTM_L = 256


def _cparams(semantics):
    return pltpu.CompilerParams(dimension_semantics=semantics, vmem_limit_bytes=VMEM_LIMIT_BYTES)


def _rms(x, g):
    ms = jnp.mean(x * x, axis=-1, keepdims=True)
    return x * lax.rsqrt(ms + RMS_EPS) * g


def _dot(a, b):
    return jnp.dot(a, b, preferred_element_type=F32)


def _dot_nt(a, b):
    return lax.dot_general(a, b, (((1,), (1,)), ((), ())), preferred_element_type=F32)


def _cast_blocks(src_refs, dst_refs):
    for src, dst in zip(src_refs, dst_refs):
        dst[...] = src[...].astype(BF16)


def _ffn_kernel(*refs, n_cast):
    x_ref, g_ref, wg_ref, wu_ref, wd_ref = refs[:5]
    o_ref = refs[5 + n_cast]
    xn_ref = refs[-1]
    f = pl.program_id(1)

    @pl.when(f == 0)
    def _():
        xn_ref[...] = _rms(x_ref[...], g_ref[...]).astype(BF16)
        o_ref[...] = jnp.zeros_like(o_ref)

    xn = xn_ref[...]
    gate = _dot(xn, wg_ref[...])
    up = _dot(xn, wu_ref[...])
    act = (gate * jax.nn.sigmoid(gate) * up).astype(BF16)
    o_ref[...] += _dot(act, wd_ref[...])
    _cast_blocks(refs[5:5 + n_cast], refs[6 + n_cast:6 + 2 * n_cast])

    @pl.when(f == pl.num_programs(1) - 1)
    def _():
        o_ref[...] = x_ref[...] + 0.5 * o_ref[...]


def _ffn(h, g, w_up, w_down, layer, next_weights=None):
    m = h.shape[0]
    nm, nf = m // TM_BIG, D_FF // TF
    in_specs = [
        pl.BlockSpec((TM_BIG, D_MODEL), lambda i, f: (i, 0)),
        pl.BlockSpec((None, 1, D_MODEL), lambda i, f: (layer, 0, 0)),
        pl.BlockSpec((D_MODEL, TF), lambda i, f: (0, f)),
        pl.BlockSpec((D_MODEL, TF), lambda i, f: (0, f + nf)),
        pl.BlockSpec((TF, D_MODEL), lambda i, f: (f, 0)),
    ]
    out_specs = [pl.BlockSpec((TM_BIG, D_MODEL), lambda i, f: (i, 0))]
    out_shape = [jax.ShapeDtypeStruct((m, D_MODEL), F32)]
    inputs = [h, g, w_up, w_up, w_down]
    if next_weights is not None:
        src_up, src_down, nl = next_weights
        up_blk = (D_MODEL // nm, 2 * D_FF // nf)
        down_blk = (D_FF // nf, D_MODEL // nm)
        in_specs += [
            pl.BlockSpec((None, *up_blk), lambda i, f: (nl, i, f)),
            pl.BlockSpec((None, *down_blk), lambda i, f: (nl, f, i)),
        ]
        out_specs += [pl.BlockSpec(up_blk, lambda i, f: (i, f)), pl.BlockSpec(down_blk, lambda i, f: (f, i))]
        out_shape += [jax.ShapeDtypeStruct(src_up.shape[1:], BF16), jax.ShapeDtypeStruct(src_down.shape[1:], BF16)]
        inputs += [src_up, src_down]
    outs = pl.pallas_call(
        functools.partial(_ffn_kernel, n_cast=len(inputs) - 5),
        grid=(nm, nf),
        in_specs=in_specs,
        out_specs=out_specs,
        out_shape=out_shape,
        scratch_shapes=[pltpu.VMEM((TM_BIG, D_MODEL), BF16)],
        compiler_params=_cparams(("parallel", "arbitrary")),
        name="ffn",
    )(*inputs)
    return outs[0], tuple(outs[1:])


def _inproj_kernel(x_ref, g_ref, w_ref, wgate_t_ref, *refs):
    n_cast = (len(refs) - 3) // 2
    z_ref, gates_t_ref = refs[n_cast:n_cast + 2]
    xn_ref = refs[-1]
    n = pl.program_id(1)

    @pl.when(n == 0)
    def _():
        xn = _rms(x_ref[...], g_ref[...]).astype(BF16)
        xn_ref[...] = xn
        gates_t_ref[...] = _dot_nt(wgate_t_ref[...].astype(BF16), xn)

    z_ref[...] = _dot_nt(xn_ref[...], w_ref[...]).astype(z_ref.dtype)
    _cast_blocks(refs[:n_cast], refs[n_cast + 2:2 * n_cast + 2])


def _inproj(h, g, w_in_t, w_gate_t, layer, later_weights):
    m = h.shape[0]
    nm, nn = m // TM_BIG, D_Z // TN_IN
    in_specs = [
        pl.BlockSpec((TM_BIG, D_MODEL), lambda i, n: (i, 0)),
        pl.BlockSpec((None, 1, D_MODEL), lambda i, n: (layer, 0, 0)),
        pl.BlockSpec((TN_IN, D_MODEL), lambda i, n: (n, 0)),
        pl.BlockSpec((None, LANES, D_MODEL), lambda i, n: (layer, 0, 0)),
    ]
    out_specs = [
        pl.BlockSpec((TM_BIG, TN_IN), lambda i, n: (i, n)),
        pl.BlockSpec((LANES, TM_BIG), lambda i, n: (0, i)),
    ]
    out_shape = [
        jax.ShapeDtypeStruct((m, D_Z), BF16),
        jax.ShapeDtypeStruct((LANES, m), F32),
    ]
    for w in later_weights:
        blk = (w.shape[1] // nm, w.shape[2] // nn)
        in_specs.append(pl.BlockSpec((None, *blk), lambda i, n: (layer, i, n)))
        out_specs.append(pl.BlockSpec(blk, lambda i, n: (i, n)))
        out_shape.append(jax.ShapeDtypeStruct(w.shape[1:], BF16))
    outs = pl.pallas_call(
        _inproj_kernel,
        grid=(nm, nn),
        in_specs=in_specs,
        out_specs=out_specs,
        out_shape=out_shape,
        scratch_shapes=[pltpu.VMEM((TM_BIG, D_MODEL), BF16)],
        compiler_params=_cparams(("parallel", "arbitrary")),
        name="inproj",
    )(h, g, w_in_t, w_gate_t, *later_weights)
    return outs[0], outs[1], tuple(outs[2:])


def _head_cols(h):
    return slice(h * HEAD_DIM, (h + 1) * HEAD_DIM)


def _stage_values(v_ref, vt_ref, nblk):
    for h in range(vt_ref.shape[0]):
        for j in range(nblk):
            vj = v_ref[j * ATT_BLOCK:(j + 1) * ATT_BLOCK, _head_cols(h)].astype(F32)
            vt_ref[h, :, j * ATT_BLOCK:(j + 1) * ATT_BLOCK] = vj.T.astype(BF16)


def _local_distance():
    qi = lax.broadcasted_iota(jnp.int32, (ATT_BLOCK, ATT_BLOCK), 1)
    ki = lax.broadcasted_iota(jnp.int32, (ATT_BLOCK, ATT_BLOCK), 0)
    return (qi - ki).astype(F32)


def _fold_sublanes(x, op):
    return op(x.reshape(ATT_BLOCK // SUBLANES, SUBLANES, x.shape[-1]), axis=0)


def _softmax_pv(cc, score_fns, s_ref, p_ref, vt_ref, o_ref):
    nk = (cc + 1) * ATT_BLOCK
    blocks = [slice(j * ATT_BLOCK, (j + 1) * ATT_BLOCK) for j in range(cc + 1)]
    maxima = []
    for h, score_fn in enumerate(score_fns):
        m8 = None
        for j, blk in enumerate(blocks):
            s = score_fn(j)
            s_ref[h, blk, :] = s
            part = _fold_sublanes(s, jnp.max)
            m8 = part if m8 is None else jnp.maximum(m8, part)
        maxima.append(jnp.max(m8, axis=0, keepdims=True))
    for h, m in enumerate(maxima):
        l8 = None
        for blk in blocks:
            p = jnp.exp2(s_ref[h, blk, :] - m)
            p_ref[h, blk, :] = p.astype(BF16)
            part = _fold_sublanes(p, jnp.sum)
            l8 = part if l8 is None else l8 + part
        l = jnp.sum(l8, axis=0, keepdims=True)
        acc = _dot(vt_ref[h, :, 0:nk], p_ref[h, 0:nk, :])
        o_ref[:, h * HEAD_DIM:(h + 1) * HEAD_DIM] = (acc / l).T.astype(o_ref.dtype)


def _attn_kernel(slope_ref, q_ref, k_ref, v_ref, mask_ref, o_ref, vt_ref, s_ref, p_ref, bias_ref, kmean_ref, *,
                 nblk, topk):
    c = pl.program_id(1)
    nh = vt_ref.shape[0]
    qk_scale = HEAD_DIM ** -0.5 * LOG2_E

    @pl.when(c == 0)
    def _():
        _stage_values(v_ref, vt_ref, nblk)
        dist = _local_distance()
        for h in range(nh):
            slope2 = slope_ref[h] * LOG2_E
            for diff in range(nblk):
                bias_ref[h, diff] = mask_ref[diff] - slope2 * (dist + float(diff * ATT_BLOCK))
            if topk is not None:
                for j in range(nblk):
                    kj = k_ref[j * ATT_BLOCK:(j + 1) * ATT_BLOCK, _head_cols(h)].astype(F32)
                    kmean_ref[h, j:j + 1, :] = jnp.mean(kj, axis=0, keepdims=True)

    def head_scores(cc, h):
        q = q_ref[:, _head_cols(h)]
        if topk is not None and cc > topk:
            km = kmean_ref[h]
            km_hi = km.astype(BF16)
            km_lo = (km - km_hi.astype(F32)).astype(BF16)
            gate = _dot_nt(km_hi, q) + _dot_nt(km_lo, q)
            g = [gate[n:n + 1, :] for n in range(cc)]
            sel = []
            for n in range(cc):
                n_ahead = jnp.zeros_like(g[n])
                for mth in range(cc):
                    if mth != n:
                        ahead = (g[mth] >= g[n]) if mth < n else (g[mth] > g[n])
                        n_ahead = n_ahead + jnp.where(ahead, 1.0, 0.0)
                sel.append(jnp.where(n_ahead < topk, 0.0, NEG_INF))
        else:
            sel = None

        def score(j):
            kj = k_ref[j * ATT_BLOCK:(j + 1) * ATT_BLOCK, _head_cols(h)]
            s = _dot_nt(kj, q) * qk_scale + bias_ref[h, cc - j]
            return s if (sel is None or j == cc) else s + sel[j]

        return score

    def branch(cc):
        _softmax_pv(cc, [head_scores(cc, h) for h in range(nh)], s_ref, p_ref, vt_ref, o_ref)

    for cc in range(nblk):
        pl.when(c == cc)(functools.partial(branch, cc))


def _block_distances(seq):
    nblk = seq // ATT_BLOCK
    kl = np.arange(ATT_BLOCK)[None, :, None]
    ql = np.arange(ATT_BLOCK)[None, None, :]
    return np.arange(nblk)[:, None, None] * ATT_BLOCK + ql - kl


def _causal_mask(seq):
    return np.where(_block_distances(seq) >= 0, 0.0, NEG_INF).astype(np.float32)


def _dilated_window_log_counts(seq):
    d = _block_distances(seq)
    cnt = np.zeros(d.shape, np.float64)
    for window, dil in DIL_PAIRS:
        cnt += (d >= 0) & (d <= window) & (d % dil == 0)
    return np.where(cnt > 0, np.log2(np.maximum(cnt, 1.0)), NEG_INF).astype(np.float32)


def _attention(z3, slopes, mask, col0, topk, name):
    b, t, _ = z3.shape
    nblk = t // ATT_BLOCK
    nh = slopes.shape[0]
    width = nh * HEAD_DIM
    cb = col0 // width
    return pl.pallas_call(
        functools.partial(_attn_kernel, nblk=nblk, topk=topk),
        grid=(b, nblk),
        in_specs=[
            pl.BlockSpec(memory_space=pltpu.SMEM),
            pl.BlockSpec((None, ATT_BLOCK, width), lambda i, c: (i, c, cb)),
            pl.BlockSpec((None, t, width), lambda i, c: (i, 0, cb + 1)),
            pl.BlockSpec((None, t, width), lambda i, c: (i, 0, cb + 2)),
            pl.BlockSpec((nblk, ATT_BLOCK, ATT_BLOCK), lambda i, c: (0, 0, 0)),
        ],
        out_specs=pl.BlockSpec((None, ATT_BLOCK, width), lambda i, c: (i, c, 0)),
        out_shape=jax.ShapeDtypeStruct((b, t, width), BF16),
        scratch_shapes=[
            pltpu.VMEM((nh, HEAD_DIM, t), BF16),
            pltpu.VMEM((nh, t, ATT_BLOCK), F32),
            pltpu.VMEM((nh, t, ATT_BLOCK), BF16),
            pltpu.VMEM((nh, nblk, ATT_BLOCK, ATT_BLOCK), F32),
            pltpu.VMEM((nh, nblk, HEAD_DIM), F32),
        ],
        compiler_params=_cparams(("parallel", "arbitrary")),
        name=name,
    )(slopes, z3, z3, z3, mask)


def _log_sigmoid(x):
    return jnp.minimum(x, 0.0) - jnp.log(1.0 + jnp.exp(-jnp.abs(x)))


def _split3(x):
    hi = x.astype(BF16)
    r1 = x - hi.astype(F32)
    mid = r1.astype(BF16)
    lo = (r1 - mid.astype(F32)).astype(BF16)
    return hi, mid, lo


def _mlstm_chunk(rows, qk_ref, v_ref, og_ref, gates_t_ref, gbias_ref, gbias_t_ref, cw_ref, cb_ref,
                 ghead_ref, tril_ref, triu_ref, shift_ref, emit, fillers, ext_ref, qk_act_ref, c_ref, n_ref, m_ref):
    L = MLSTM_L
    nqk = H_MLSTM * DK_MLSTM
    fill = iter(fillers)
    next(fill)()

    u_bf = qk_ref[rows, :]
    u = u_bf.astype(F32)
    shifted = _dot(shift_ref[...], u_bf)
    cw = cw_ref[...]
    w_rows = [cw[CONV_WIDTH - 1 - back:CONV_WIDTH - back, :] for back in range(CONV_WIDTH)]
    y = cb_ref[...] + w_rows[0] * u
    ext_ref[SUBLANES:2 * SUBLANES, :] = u[0:SUBLANES, :]
    y_head = cb_ref[...] + w_rows[0] * u[0:SUBLANES, :]
    for back in range(1, CONV_WIDTH):
        y = y + w_rows[back] * shifted[(back - 1) * L:back * L, :]
        y_head = y_head + w_rows[back] * ext_ref[SUBLANES - back:2 * SUBLANES - back, :]
    ext_ref[0:SUBLANES, :] = u[L - SUBLANES:L, :]
    qk_act_ref[...] = y * jax.nn.sigmoid(y)
    qk_act_ref[0:SUBLANES, :] = y_head * jax.nn.sigmoid(y_head)

    next(fill)()
    gates_t = gates_t_ref[:, rows]
    g_row = gates_t[0:N_GATES, :] + gbias_t_ref[...]
    g_col = gates_t.T + gbias_ref[...]
    lf_col = _log_sigmoid(g_col)
    lf_row = _log_sigmoid(g_row)
    b_col_all = sum(_dot(tril_ref[...], t) for t in _split3(lf_col))
    b_row_all = sum(_dot(t, triu_ref[...]) for t in _split3(lf_row))

    ti = lax.broadcasted_iota(jnp.int32, (L, L), 0)
    si = lax.broadcasted_iota(jnp.int32, (L, L), 1)
    causal = si <= ti

    heads = []
    for h in range(H_MLSTM):
        q = qk_act_ref[:, h * DK_MLSTM:(h + 1) * DK_MLSTM]
        k = qk_act_ref[:, nqk + h * DK_MLSTM:nqk + (h + 1) * DK_MLSTM] * (DK_MLSTM ** -0.5)
        qb = q.astype(BF16)
        li_col = g_col[:, h:h + 1]
        li_row = g_row[h:h + 1, :]
        b_col = b_col_all[:, H_MLSTM + h:H_MLSTM + h + 1]
        b_row = b_row_all[H_MLSTM + h:H_MLSTM + h + 1, :]
        m_st = m_ref[h]
        d_log = jnp.where(causal, b_col - b_row + li_row, NEG_INF)
        m_inter = b_col + m_st
        m_t = jnp.maximum(m_inter, jnp.max(d_log, axis=-1, keepdims=True))
        s = _dot_nt(qb, k.astype(BF16)) * jnp.exp(d_log - m_t)
        heads.append(dict(q=q, k=k, qb=qb, s=s, m_t=m_t, w_inter=jnp.exp(m_inter - m_t), m_st=m_st,
                          c_st=c_ref[h], n_st=n_ref[h],
                          li_col=li_col, li_row=li_row, b_col=b_col, b_row=b_row))

    next(fill)()
    for h, hd in enumerate(heads):
        v = v_ref[rows, h * DV_MLSTM:(h + 1) * DV_MLSTM]
        b_last = hd["b_col"][L - 1:L, :]
        g_dec_row = b_last - hd["b_row"] + hd["li_row"]
        g_dec_col = b_last - hd["b_col"] + hd["li_col"]
        m_new = jnp.maximum(b_last + hd["m_st"], jnp.max(g_dec_row, axis=-1, keepdims=True))
        w_old = jnp.exp(b_last + hd["m_st"] - m_new)
        kw = hd["k"] * jnp.exp(g_dec_col - m_new)
        c_ref[h] = w_old * hd["c_st"] + _dot(kw.T.astype(BF16), v)
        n_ref[h] = w_old * hd["n_st"] + jnp.sum(kw, axis=0, keepdims=True)
        m_ref[h] = m_new

    next(fill)()
    for h, hd in enumerate(heads):
        cols = slice(h * DV_MLSTM, (h + 1) * DV_MLSTM)
        v = v_ref[rows, cols]
        num = hd["w_inter"] * _dot(hd["qb"], hd["c_st"].astype(BF16)) + _dot(hd["s"].astype(BF16), v)
        den = (hd["w_inter"] * jnp.sum(hd["q"] * hd["n_st"], axis=-1, keepdims=True)
               + jnp.sum(hd["s"], axis=-1, keepdims=True))
        hh = num / jnp.maximum(jnp.abs(den), jnp.exp(-hd["m_t"]))
        hh = hh * lax.rsqrt(jnp.mean(hh * hh, axis=-1, keepdims=True) + RMS_EPS)
        hh = hh * ghead_ref[:, cols]
        emit(cols, jax.nn.sigmoid(og_ref[rows, cols].astype(F32)) * hh)


def _conv_shift_matrices(length):
    t = np.arange(length)
    mats = [(t[:, None] - t[None, :] == s).astype(np.float32) for s in range(1, CONV_WIDTH)]
    return np.concatenate(mats, axis=0)


def _mixout_kernel(h_ref, a_ref, cd_ref, w_ref, qk_ref, v_ref, og_ref, gates_t_ref, gbias_ref, gbias_t_ref, cw_ref,
                   cb_ref, ghead_ref, tril_ref, triu_ref, shift_ref, o_ref, mo_ref, ext_ref, qk_act_ref,
                   c_ref, n_ref, m_ref, *, n_tiles, tiles_per_seq):
    s = pl.program_id(0)
    tile = jnp.minimum(s, n_tiles - 1)

    @pl.when(s == 0)
    def _():
        mo_ref[...] = jnp.zeros_like(mo_ref)

    @pl.when(tile % tiles_per_seq == 0)
    def _():
        c_ref[...] = jnp.zeros_like(c_ref)
        n_ref[...] = jnp.zeros_like(n_ref)
        m_ref[...] = jnp.zeros_like(m_ref)
        ext_ref[0:SUBLANES, :] = jnp.zeros((SUBLANES, ext_ref.shape[1]), F32)

    na = a_ref.shape[1]
    nm = mo_ref.shape[2]
    n_chunks = TM // MLSTM_L
    n_pieces = 4 * n_chunks
    width = o_ref.shape[1] // n_pieces

    def project(piece):
        cols = slice(piece * width, (piece + 1) * width)
        acc = _dot(a_ref[...], w_ref[0:na, cols])
        acc += _dot(mo_ref[(s + 1) % 2], w_ref[na:na + nm, cols])
        acc += _dot(cd_ref[...], w_ref[na + nm:, cols])
        o_ref[:, cols] = h_ref[:, cols] + acc

    slot = s % 2
    for k in range(n_chunks):
        rows = slice(k * MLSTM_L, (k + 1) * MLSTM_L)

        def emit(cols, value, rows=rows):
            mo_ref[slot, rows, cols] = value.astype(mo_ref.dtype)

        fillers = [functools.partial(project, 4 * k + i) for i in range(4)]
        _mlstm_chunk(rows, qk_ref, v_ref, og_ref, gates_t_ref, gbias_ref, gbias_t_ref, cw_ref, cb_ref, ghead_ref,
                     tril_ref, triu_ref, shift_ref, emit, fillers, ext_ref, qk_act_ref, c_ref, n_ref, m_ref)


def _mixout(h, z, out_a, out_c, w_out, gates_t, gbias, gbias_t, conv_w, conv_b, g_head, tril, triu, shifts,
            layer, seq):
    m = h.shape[0]
    L = MLSTM_L
    nt = m // TM
    nqk = 2 * H_MLSTM * DK_MLSTM
    nv = H_MLSTM * DV_MLSTM
    na, ncol = out_a.shape[1], out_c.shape[1]
    qk_blk = (2 * D_ATT) // nqk
    v_blk = (2 * D_ATT + nqk) // nv
    og_blk = v_blk + 1
    prev = lambda s: (jnp.maximum(s - 1, 0), 0)
    cur = lambda s: jnp.minimum(s, nt - 1)
    const = lambda s: (0, 0)
    return pl.pallas_call(
        functools.partial(_mixout_kernel, n_tiles=nt, tiles_per_seq=seq // TM),
        grid=(nt + 1,),
        in_specs=[
            pl.BlockSpec((TM, D_MODEL), prev),
            pl.BlockSpec((TM, na), prev),
            pl.BlockSpec((TM, ncol), prev),
            pl.BlockSpec((na + nv + ncol, D_MODEL), const),
            pl.BlockSpec((TM, nqk), lambda s: (cur(s), qk_blk)),
            pl.BlockSpec((TM, nv), lambda s: (cur(s), v_blk)),
            pl.BlockSpec((TM, nv), lambda s: (cur(s), og_blk)),
            pl.BlockSpec((LANES, TM), lambda s: (0, cur(s))),
            pl.BlockSpec((None, 1, LANES), lambda s: (layer, 0, 0)),
            pl.BlockSpec((None, N_GATES, 1), lambda s: (layer, 0, 0)),
            pl.BlockSpec((None, CONV_WIDTH, nqk), lambda s: (layer, 0, 0)),
            pl.BlockSpec((None, 1, nqk), lambda s: (layer, 0, 0)),
            pl.BlockSpec((None, 1, nv), lambda s: (layer, 0, 0)),
            pl.BlockSpec((L, L), const),
            pl.BlockSpec((L, L), const),
            pl.BlockSpec(((CONV_WIDTH - 1) * L, L), const),
        ],
        out_specs=pl.BlockSpec((TM, D_MODEL), prev),
        out_shape=jax.ShapeDtypeStruct((m, D_MODEL), F32),
        scratch_shapes=[
            pltpu.VMEM((2, TM, nv), BF16),
            pltpu.VMEM((2 * SUBLANES, nqk), F32),
            pltpu.VMEM((L, nqk), F32),
            pltpu.VMEM((H_MLSTM, DK_MLSTM, DV_MLSTM), F32),
            pltpu.VMEM((H_MLSTM, 1, DK_MLSTM), F32),
            pltpu.VMEM((H_MLSTM, 1, 1), F32),
        ],
        compiler_params=_cparams(("arbitrary",)),
        name="mixout",
    )(h, out_a, out_c, w_out, z, z, z, gates_t, gbias, gbias_t, conv_w, conv_b, g_head, tril, triu, shifts)


def _ple_kernel(h_ref, g_ref, p_ref, wg_ref, wp_ref, gfin_ref, *refs, final):
    n_cast = len(refs) // 2
    o_ref = refs[n_cast]
    x = h_ref[...]
    xn = _rms(x, g_ref[...]).astype(BF16)
    gate = jax.nn.sigmoid(_dot(xn, wg_ref[...]))
    proj = _dot(p_ref[...].astype(BF16), wp_ref[...])
    y = x + gate * proj
    if final:
        y = _rms(y, gfin_ref[...])
    o_ref[...] = y
    _cast_blocks(refs[:n_cast], refs[n_cast + 1:])


def _ple(h, g, p2, w_gate, w_proj, g_final, layer, next_w_in_t=None):
    m = h.shape[0]
    nm = m // TM
    in_specs = [
        pl.BlockSpec((TM, D_MODEL), lambda i: (i, 0)),
        pl.BlockSpec((None, 1, D_MODEL), lambda i: (layer, 0, 0)),
        pl.BlockSpec((None, TM, D_PLE), lambda i: (layer, i, 0)),
        pl.BlockSpec((D_MODEL, D_MODEL), lambda i: (0, 0)),
        pl.BlockSpec((D_PLE, D_MODEL), lambda i: (0, 0)),
        pl.BlockSpec((1, D_MODEL), lambda i: (0, 0)),
    ]
    out_specs = [pl.BlockSpec((TM, D_MODEL), lambda i: (i, 0))]
    out_shape = [jax.ShapeDtypeStruct((m, D_MODEL), F32)]
    inputs = [h, g, p2, w_gate, w_proj, g_final]
    if next_w_in_t is not None:
        src, nl = next_w_in_t
        blk = (D_Z // nm, D_MODEL)
        in_specs.append(pl.BlockSpec((None, *blk), lambda i: (nl, i, 0)))
        out_specs.append(pl.BlockSpec(blk, lambda i: (i, 0)))
        out_shape.append(jax.ShapeDtypeStruct((D_Z, D_MODEL), BF16))
        inputs.append(src)
    outs = pl.pallas_call(
        functools.partial(_ple_kernel, final=next_w_in_t is None),
        grid=(nm,),
        in_specs=in_specs,
        out_specs=out_specs,
        out_shape=out_shape,
        compiler_params=_cparams(("parallel",)),
        name="ple",
    )(*inputs)
    return outs[0], (outs[1] if len(outs) > 1 else None)


def _cast_kernel(src_ref, dst_ref):
    dst_ref[...] = src_ref[...].astype(BF16)


def _cast_rows(src, layer, rows):
    cols = src.shape[2]
    return pl.pallas_call(
        _cast_kernel,
        grid=(rows // TM,),
        in_specs=[pl.BlockSpec((None, TM, cols), lambda i: (layer, i, 0))],
        out_specs=pl.BlockSpec((TM, cols), lambda i: (i, 0)),
        out_shape=jax.ShapeDtypeStruct((rows, cols), BF16),
        compiler_params=_cparams(("parallel",)),
        name="cast_rows",
    )(src)


def kernel(x, p, g_ffn1, w_up1, w_down1, g_mix, w_in, conv_w, conv_b, b_igate, b_fgate, g_head, w_out, g_ffn2, w_up2, w_down2, g_ple, w_ple_gate, w_ple_proj, g_final):
    b, t, d = x.shape
    depth = p.shape[0]
    m = b * t

    ffn_w = (w_up1[0].astype(BF16), w_down1[0].astype(BF16))
    w_in_t = jnp.swapaxes(w_in, 1, 2)
    w_in_tb = _cast_rows(w_in_t, 0, D_Z)
    w_gate_t = jnp.pad(w_in_t[:, D_Z:, :], ((0, 0), (0, LANES - N_GATES), (0, 0)))
    gate_bias = jnp.concatenate([b_igate, b_fgate], axis=-1)
    gbias_col = jnp.pad(gate_bias, ((0, 0), (0, LANES - N_GATES)))[:, None, :]
    gbias_row = gate_bias[:, :, None]
    row = lambda a: a[:, None, :]
    g_ffn1r, g_mixr, g_ffn2r, g_pler = row(g_ffn1), row(g_mix), row(g_ffn2), row(g_ple)
    conv_br, g_headr = row(conv_b), row(g_head)
    g_finalr = g_final[None, :]
    p2 = p.reshape(depth, m, D_PLE)

    slopes = 2.0 ** (-8.0 * jnp.arange(1, N_ALIBI + 1, dtype=F32) / N_ALIBI)
    slopes_moba, slopes_dil = slopes[0::2], slopes[1::2]
    causal_mask = jnp.asarray(_causal_mask(t))
    log_counts = jnp.asarray(_dilated_window_log_counts(t))
    tri = np.tril(np.ones((MLSTM_L, MLSTM_L), np.float32))
    tril, triu = jnp.asarray(tri, BF16), jnp.asarray(tri.T, BF16)
    shifts = jnp.asarray(_conv_shift_matrices(MLSTM_L), BF16)

    h = x.reshape(m, d)
    for i in range(depth):
        h, ffn_w = _ffn(h, g_ffn1r, *ffn_w, i, next_weights=(w_up2, w_down2, i))
        z, gates_t, (w_outb, w_pgb, w_ppb) = _inproj(
            h, g_mixr, w_in_tb, w_gate_t, i, (w_out, w_ple_gate, w_ple_proj))
        z3 = z.reshape(b, t, D_Z)
        out_a = _attention(z3, slopes_moba, causal_mask, 0, MOBA_TOPK, "moba")
        out_c = _attention(z3, slopes_dil, log_counts, D_ATT, None, "dilated")
        h = _mixout(h, z, out_a.reshape(m, -1), out_c.reshape(m, -1), w_outb, gates_t, gbias_col, gbias_row,
                    conv_w, conv_br, g_headr, tril, triu, shifts, i, t)
        upcoming = (w_up1, w_down1, i + 1) if i + 1 < depth else None
        h, ffn_w = _ffn(h, g_ffn2r, *ffn_w, i, next_weights=upcoming)
        h, w_in_tb = _ple(h, g_pler, p2, w_pgb, w_ppb, g_finalr, i,
                          next_w_in_t=(w_in_t, i + 1) if i + 1 < depth else None)
    return h.reshape(b, t, d)
```

```python
import functools

import numpy as np
import jax
import jax.numpy as jnp
from jax import lax
from jax.experimental import pallas as pl
from jax.experimental.pallas import tpu as pltpu

F32 = jnp.float32
BF16 = jnp.bfloat16

D_MODEL = 2048
DEPTH = 4
HEAD_DIM = 128
H_MOBA = 4
H_MLSTM = 4
H_DIL = 4
DK_MLSTM = 128
DV_MLSTM = 256
D_FF = 5632
D_PLE = 256
MOBA_BLOCK = 256
MOBA_TOPK = 3
DIL_PAIRS = ((128, 1), (512, 4), (2048, 16))
CONV_WIDTH = 4
RMS_EPS = 1e-6
LOG2_E = 1.4426950408889634
NEG_INF = -1e30
N_ALIBI = H_MOBA + H_DIL
D_ATT = 3 * H_MOBA * HEAD_DIM
D_Z = 2 * D_ATT + 2 * H_MLSTM * DK_MLSTM + 2 * H_MLSTM * DV_MLSTM
N_GATES = 2 * H_MLSTM

LANES = 128
SUBLANES = 8
VMEM_LIMIT_BYTES = 60 * 1024 * 1024

TM = 512
TM_BIG = 1024
TF = 512
FFN_RESIDUAL_PIECES = 8
TN_IN = 1536
ATT_BLOCK = 256
MLSTM_L = 256


def _cparams(semantics):
    return pltpu.CompilerParams(dimension_semantics=semantics, vmem_limit_bytes=VMEM_LIMIT_BYTES)


def _rms(x, g):
    ms = jnp.mean(x * x, axis=-1, keepdims=True)
    return x * lax.rsqrt(ms + RMS_EPS) * g


def _dot(a, b):
    return jnp.dot(a, b, preferred_element_type=F32)


def _dot_nt(a, b):
    return lax.dot_general(a, b, (((1,), (1,)), ((), ())), preferred_element_type=F32)


def _cast_blocks(src_refs, dst_refs):
    for src, dst in zip(src_refs, dst_refs):
        dst[...] = src[...].astype(BF16)


def _ffn_kernel(*refs, n_cast):
    x_ref, g_ref, wg_ref, wu_ref, wd_ref = refs[:5]
    o_ref = refs[5 + n_cast]
    xn_ref = refs[-1]
    f = pl.program_id(1)
    last = pl.num_programs(1) - 1

    def step(first=False, final=False):
        if first:
            xn_ref[...] = _rms(x_ref[...], g_ref[...]).astype(BF16)
        xn = xn_ref[...]
        gate = _dot(xn, wg_ref[...])
        up = _dot(xn, wu_ref[...])
        act = (gate * jax.nn.sigmoid(gate) * up).astype(BF16)
        if first:
            o_ref[...] = _dot(act, wd_ref[...])
        elif final:
            width = o_ref.shape[1] // FFN_RESIDUAL_PIECES
            for j in range(FFN_RESIDUAL_PIECES):
                cols = slice(j * width, (j + 1) * width)
                total = o_ref[:, cols] + _dot(act, wd_ref[:, cols])
                o_ref[:, cols] = x_ref[:, cols] + 0.5 * total
        else:
            o_ref[...] += _dot(act, wd_ref[...])
        _cast_blocks(refs[5:5 + n_cast], refs[6 + n_cast:6 + 2 * n_cast])

    pl.when(f == 0)(functools.partial(step, first=True))
    pl.when((f > 0) & (f < last))(step)
    pl.when(f == last)(functools.partial(step, final=True))


def _ffn(h, g, w_up, w_down, layer, next_weights=None):
    m = h.shape[0]
    nm, nf = m // TM_BIG, D_FF // TF
    in_specs = [
        pl.BlockSpec((TM_BIG, D_MODEL), lambda i, f: (i, 0)),
        pl.BlockSpec((None, 1, D_MODEL), lambda i, f: (layer, 0, 0)),
        pl.BlockSpec((D_MODEL, TF), lambda i, f: (0, f)),
        pl.BlockSpec((D_MODEL, TF), lambda i, f: (0, f + nf)),
        pl.BlockSpec((TF, D_MODEL), lambda i, f: (f, 0)),
    ]
    out_specs = [pl.BlockSpec((TM_BIG, D_MODEL), lambda i, f: (i, 0))]
    out_shape = [jax.ShapeDtypeStruct((m, D_MODEL), F32)]
    inputs = [h, g, w_up, w_up, w_down]
    if next_weights is not None:
        src_up, src_down, nl = next_weights
        up_blk = (D_MODEL // nm, 2 * D_FF // nf)
        down_blk = (D_FF // nf, D_MODEL // nm)
        in_specs += [
            pl.BlockSpec((None, *up_blk), lambda i, f: (nl, i, f)),
            pl.BlockSpec((None, *down_blk), lambda i, f: (nl, f, i)),
        ]
        out_specs += [pl.BlockSpec(up_blk, lambda i, f: (i, f)), pl.BlockSpec(down_blk, lambda i, f: (f, i))]
        out_shape += [jax.ShapeDtypeStruct(src_up.shape[1:], BF16), jax.ShapeDtypeStruct(src_down.shape[1:], BF16)]
        inputs += [src_up, src_down]
    outs = pl.pallas_call(
        functools.partial(_ffn_kernel, n_cast=len(inputs) - 5),
        grid=(nm, nf),
        in_specs=in_specs,
        out_specs=out_specs,
        out_shape=out_shape,
        scratch_shapes=[pltpu.VMEM((TM_BIG, D_MODEL), BF16)],
        compiler_params=_cparams(("parallel", "arbitrary")),
        name="ffn",
    )(*inputs)
    return outs[0], tuple(outs[1:])


def _inproj_kernel(x_ref, g_ref, w_ref, wgate_t_ref, *refs):
    n_cast = (len(refs) - 3) // 2
    z_ref, gates_t_ref = refs[n_cast:n_cast + 2]
    xn_ref = refs[-1]
    n = pl.program_id(1)

    def step(first=False):
        if first:
            xn_ref[...] = _rms(x_ref[...], g_ref[...]).astype(BF16)
        xn = xn_ref[...]
        z_ref[...] = _dot_nt(xn, w_ref[...]).astype(z_ref.dtype)
        if first:
            gates_t_ref[...] = _dot_nt(wgate_t_ref[...].astype(BF16), xn)
        _cast_blocks(refs[:n_cast], refs[n_cast + 2:2 * n_cast + 2])

    pl.when(n == 0)(functools.partial(step, first=True))
    pl.when(n > 0)(step)


def _inproj(h, g, w_in_t, w_gate_t, layer, later_weights):
    m = h.shape[0]
    nm, nn = m // TM_BIG, D_Z // TN_IN
    in_specs = [
        pl.BlockSpec((TM_BIG, D_MODEL), lambda i, n: (i, 0)),
        pl.BlockSpec((None, 1, D_MODEL), lambda i, n: (layer, 0, 0)),
        pl.BlockSpec((TN_IN, D_MODEL), lambda i, n: (n, 0)),
        pl.BlockSpec((None, LANES, D_MODEL), lambda i, n: (layer, 0, 0)),
    ]
    out_specs = [
        pl.BlockSpec((TM_BIG, TN_IN), lambda i, n: (i, n)),
        pl.BlockSpec((LANES, TM_BIG), lambda i, n: (0, i)),
    ]
    out_shape = [
        jax.ShapeDtypeStruct((m, D_Z), BF16),
        jax.ShapeDtypeStruct((LANES, m), F32),
    ]
    for w in later_weights:
        blk = (w.shape[1] // nm, w.shape[2] // nn)
        in_specs.append(pl.BlockSpec((None, *blk), lambda i, n: (layer, i, n)))
        out_specs.append(pl.BlockSpec(blk, lambda i, n: (i, n)))
        out_shape.append(jax.ShapeDtypeStruct(w.shape[1:], BF16))
    outs = pl.pallas_call(
        _inproj_kernel,
        grid=(nm, nn),
        in_specs=in_specs,
        out_specs=out_specs,
        out_shape=out_shape,
        scratch_shapes=[pltpu.VMEM((TM_BIG, D_MODEL), BF16)],
        compiler_params=_cparams(("parallel", "arbitrary")),
        name="inproj",
    )(h, g, w_in_t, w_gate_t, *later_weights)
    return outs[0], outs[1], tuple(outs[2:])


def _head_cols(h):
    return slice(h * HEAD_DIM, (h + 1) * HEAD_DIM)


def _stage_values(v_ref, vt_ref, nblk):
    for h in range(vt_ref.shape[0]):
        for j in range(nblk):
            vj = v_ref[j * ATT_BLOCK:(j + 1) * ATT_BLOCK, _head_cols(h)].astype(F32)
            vt_ref[h, :, j * ATT_BLOCK:(j + 1) * ATT_BLOCK] = vj.T.astype(BF16)


def _local_distance():
    qi = lax.broadcasted_iota(jnp.int32, (ATT_BLOCK, ATT_BLOCK), 1)
    ki = lax.broadcasted_iota(jnp.int32, (ATT_BLOCK, ATT_BLOCK), 0)
    return (qi - ki).astype(F32)


def _fold_sublanes(x, op):
    return op(x.reshape(ATT_BLOCK // SUBLANES, SUBLANES, x.shape[-1]), axis=0)


def _softmax_pv(cc, score_fns, s_ref, p_ref, vt_ref, o_ref):
    nk = (cc + 1) * ATT_BLOCK
    blocks = [slice(j * ATT_BLOCK, (j + 1) * ATT_BLOCK) for j in range(cc + 1)]
    maxima = []
    for h, score_fn in enumerate(score_fns):
        m8 = None
        for j, blk in enumerate(blocks):
            s = score_fn(j)
            s_ref[h, blk, :] = s
            part = _fold_sublanes(s, jnp.max)
            m8 = part if m8 is None else jnp.maximum(m8, part)
        maxima.append(jnp.max(m8, axis=0, keepdims=True))
    for h, m in enumerate(maxima):
        l8 = None
        for blk in blocks:
            p = jnp.exp2(s_ref[h, blk, :] - m)
            p_ref[h, blk, :] = p.astype(BF16)
            part = _fold_sublanes(p, jnp.sum)
            l8 = part if l8 is None else l8 + part
        l = jnp.sum(l8, axis=0, keepdims=True)
        acc = _dot(vt_ref[h, :, 0:nk], p_ref[h, 0:nk, :])
        o_ref[:, h * HEAD_DIM:(h + 1) * HEAD_DIM] = (acc / l).T.astype(o_ref.dtype)


def _attn_kernel(slope_ref, q_ref, k_ref, v_ref, mask_ref, o_ref, vt_ref, s_ref, p_ref, bias_ref, kmean_ref, *,
                 nblk, topk):
    c = pl.program_id(1)
    nh = vt_ref.shape[0]
    qk_scale = HEAD_DIM ** -0.5 * LOG2_E

    @pl.when(c == 0)
    def _():
        _stage_values(v_ref, vt_ref, nblk)
        dist = _local_distance()
        for h in range(nh):
            slope2 = slope_ref[h] * LOG2_E
            for diff in range(nblk):
                bias_ref[h, diff] = mask_ref[diff] - slope2 * (dist + float(diff * ATT_BLOCK))
            if topk is not None:
                for j in range(nblk):
                    kj = k_ref[j * ATT_BLOCK:(j + 1) * ATT_BLOCK, _head_cols(h)].astype(F32)
                    kmean_ref[h, j:j + 1, :] = jnp.mean(kj, axis=0, keepdims=True)

    def head_scores(cc, h):
        q = q_ref[:, _head_cols(h)]
        if topk is not None and cc > topk:
            km = kmean_ref[h]
            km_hi = km.astype(BF16)
            km_lo = (km - km_hi.astype(F32)).astype(BF16)
            gate = _dot_nt(km_hi, q) + _dot_nt(km_lo, q)
            g = [gate[n:n + 1, :] for n in range(cc)]
            sel = []
            for n in range(cc):
                n_ahead = jnp.zeros_like(g[n])
                for mth in range(cc):
                    if mth != n:
                        ahead = (g[mth] >= g[n]) if mth < n else (g[mth] > g[n])
                        n_ahead = n_ahead + jnp.where(ahead, 1.0, 0.0)
                sel.append(jnp.where(n_ahead < topk, 0.0, NEG_INF))
        else:
            sel = None

        def score(j):
            kj = k_ref[j * ATT_BLOCK:(j + 1) * ATT_BLOCK, _head_cols(h)]
            s = _dot_nt(kj, q) * qk_scale + bias_ref[h, cc - j]
            return s if (sel is None or j == cc) else s + sel[j]

        return score

    def branch(cc):
        _softmax_pv(cc, [head_scores(cc, h) for h in range(nh)], s_ref, p_ref, vt_ref, o_ref)

    for cc in range(nblk):
        pl.when(c == cc)(functools.partial(branch, cc))


def _block_distances(seq):
    nblk = seq // ATT_BLOCK
    kl = np.arange(ATT_BLOCK)[None, :, None]
    ql = np.arange(ATT_BLOCK)[None, None, :]
    return np.arange(nblk)[:, None, None] * ATT_BLOCK + ql - kl


def _causal_mask(seq):
    return np.where(_block_distances(seq) >= 0, 0.0, NEG_INF).astype(np.float32)


def _dilated_window_log_counts(seq):
    d = _block_distances(seq)
    cnt = np.zeros(d.shape, np.float64)
    for window, dil in DIL_PAIRS:
        cnt += (d >= 0) & (d <= window) & (d % dil == 0)
    return np.where(cnt > 0, np.log2(np.maximum(cnt, 1.0)), NEG_INF).astype(np.float32)


def _attention(z3, slopes, mask, col0, topk, name):
    b, t, _ = z3.shape
    nblk = t // ATT_BLOCK
    nh = slopes.shape[0]
    width = nh * HEAD_DIM
    cb = col0 // width
    return pl.pallas_call(
        functools.partial(_attn_kernel, nblk=nblk, topk=topk),
        grid=(b, nblk),
        in_specs=[
            pl.BlockSpec(memory_space=pltpu.SMEM),
            pl.BlockSpec((None, ATT_BLOCK, width), lambda i, c: (i, c, cb)),
            pl.BlockSpec((None, t, width), lambda i, c: (i, 0, cb + 1)),
            pl.BlockSpec((None, t, width), lambda i, c: (i, 0, cb + 2)),
            pl.BlockSpec((nblk, ATT_BLOCK, ATT_BLOCK), lambda i, c: (0, 0, 0)),
        ],
        out_specs=pl.BlockSpec((None, ATT_BLOCK, width), lambda i, c: (i, c, 0)),
        out_shape=jax.ShapeDtypeStruct((b, t, width), BF16),
        scratch_shapes=[
            pltpu.VMEM((nh, HEAD_DIM, t), BF16),
            pltpu.VMEM((nh, t, ATT_BLOCK), F32),
            pltpu.VMEM((nh, t, ATT_BLOCK), BF16),
            pltpu.VMEM((nh, nblk, ATT_BLOCK, ATT_BLOCK), F32),
            pltpu.VMEM((nh, nblk, HEAD_DIM), F32),
        ],
        compiler_params=_cparams(("parallel", "arbitrary")),
        name=name,
    )(slopes, z3, z3, z3, mask)


def _log_sigmoid(x):
    return jnp.minimum(x, 0.0) - jnp.log(1.0 + jnp.exp(-jnp.abs(x)))


def _split3(x):
    hi = x.astype(BF16)
    r1 = x - hi.astype(F32)
    mid = r1.astype(BF16)
    lo = (r1 - mid.astype(F32)).astype(BF16)
    return hi, mid, lo


def _mlstm_chunk(rows, qk_ref, v_ref, og_ref, gates_t_ref, gbias_ref, gbias_t_ref, cw_ref, cb_ref,
                 ghead_ref, tril_ref, triu_ref, shift_ref, emit, fillers, ext_ref, qk_act_ref, c_ref, n_ref, m_ref):
    L = MLSTM_L
    nqk = H_MLSTM * DK_MLSTM
    fill = iter(fillers)
    next(fill)()

    u_bf = qk_ref[rows, :]
    u = u_bf.astype(F32)
    shifted = _dot(shift_ref[...], u_bf)
    cw = cw_ref[...]
    w_rows = [cw[CONV_WIDTH - 1 - back:CONV_WIDTH - back, :] for back in range(CONV_WIDTH)]
    y = cb_ref[...] + w_rows[0] * u
    ext_ref[SUBLANES:2 * SUBLANES, :] = u[0:SUBLANES, :]
    y_head = cb_ref[...] + w_rows[0] * u[0:SUBLANES, :]
    for back in range(1, CONV_WIDTH):
        y = y + w_rows[back] * shifted[(back - 1) * L:back * L, :]
        y_head = y_head + w_rows[back] * ext_ref[SUBLANES - back:2 * SUBLANES - back, :]
    ext_ref[0:SUBLANES, :] = u[L - SUBLANES:L, :]
    qk_act_ref[...] = y * jax.nn.sigmoid(y)
    qk_act_ref[0:SUBLANES, :] = y_head * jax.nn.sigmoid(y_head)

    next(fill)()
    gates_t = gates_t_ref[:, rows]
    g_row = gates_t[0:N_GATES, :] + gbias_t_ref[...]
    g_col = gates_t.T + gbias_ref[...]
    lf_col = _log_sigmoid(g_col)
    lf_row = _log_sigmoid(g_row)
    b_col_all = sum(_dot(tril_ref[...], t) for t in _split3(lf_col))
    b_row_all = sum(_dot(t, triu_ref[...]) for t in _split3(lf_row))

    ti = lax.broadcasted_iota(jnp.int32, (L, L), 0)
    si = lax.broadcasted_iota(jnp.int32, (L, L), 1)
    causal = si <= ti

    heads = []
    for h in range(H_MLSTM):
        q = qk_act_ref[:, h * DK_MLSTM:(h + 1) * DK_MLSTM]
        k = qk_act_ref[:, nqk + h * DK_MLSTM:nqk + (h + 1) * DK_MLSTM] * (DK_MLSTM ** -0.5)
        qb = q.astype(BF16)
        li_col = g_col[:, h:h + 1]
        li_row = g_row[h:h + 1, :]
        b_col = b_col_all[:, H_MLSTM + h:H_MLSTM + h + 1]
        b_row = b_row_all[H_MLSTM + h:H_MLSTM + h + 1, :]
        m_st = m_ref[h]
        d_log = jnp.where(causal, b_col - b_row + li_row, NEG_INF)
        m_inter = b_col + m_st
        m_t = jnp.maximum(m_inter, jnp.max(d_log, axis=-1, keepdims=True))
        s = _dot_nt(qb, k.astype(BF16)) * jnp.exp(d_log - m_t)
        heads.append(dict(q=q, k=k, qb=qb, s=s, m_t=m_t, w_inter=jnp.exp(m_inter - m_t), m_st=m_st,
                          c_st=c_ref[h], n_st=n_ref[h],
                          li_col=li_col, li_row=li_row, b_col=b_col, b_row=b_row))

    next(fill)()
    for h, hd in enumerate(heads):
        v = v_ref[rows, h * DV_MLSTM:(h + 1) * DV_MLSTM]
        b_last = hd["b_col"][L - 1:L, :]
        g_dec_row = b_last - hd["b_row"] + hd["li_row"]
        g_dec_col = b_last - hd["b_col"] + hd["li_col"]
        m_new = jnp.maximum(b_last + hd["m_st"], jnp.max(g_dec_row, axis=-1, keepdims=True))
        w_old = jnp.exp(b_last + hd["m_st"] - m_new)
        kw = hd["k"] * jnp.exp(g_dec_col - m_new)
        c_ref[h] = w_old * hd["c_st"] + _dot(kw.T.astype(BF16), v)
        n_ref[h] = w_old * hd["n_st"] + jnp.sum(kw, axis=0, keepdims=True)
        m_ref[h] = m_new

    next(fill)()
    for h, hd in enumerate(heads):
        cols = slice(h * DV_MLSTM, (h + 1) * DV_MLSTM)
        v = v_ref[rows, cols]
        num = hd["w_inter"] * _dot(hd["qb"], hd["c_st"].astype(BF16)) + _dot(hd["s"].astype(BF16), v)
        den = (hd["w_inter"] * jnp.sum(hd["q"] * hd["n_st"], axis=-1, keepdims=True)
               + jnp.sum(hd["s"], axis=-1, keepdims=True))
        hh = num / jnp.maximum(jnp.abs(den), jnp.exp(-hd["m_t"]))
        hh = hh * lax.rsqrt(jnp.mean(hh * hh, axis=-1, keepdims=True) + RMS_EPS)
        hh = hh * ghead_ref[:, cols]
        emit(cols, jax.nn.sigmoid(og_ref[rows, cols].astype(F32)) * hh)


def _conv_shift_matrices(length):
    t = np.arange(length)
    mats = [(t[:, None] - t[None, :] == s).astype(np.float32) for s in range(1, CONV_WIDTH)]
    return np.concatenate(mats, axis=0)


def _mixout_kernel(h_ref, a_ref, cd_ref, w_ref, qk_ref, v_ref, og_ref, gates_t_ref, gbias_ref, gbias_t_ref, cw_ref,
                   cb_ref, ghead_ref, tril_ref, triu_ref, shift_ref, o_ref, mo_ref, ext_ref, qk_act_ref,
                   c_ref, n_ref, m_ref, *, n_tiles, tiles_per_seq):
    s = pl.program_id(0)
    tile = jnp.minimum(s, n_tiles - 1)

    @pl.when(s == 0)
    def _():
        mo_ref[...] = jnp.zeros_like(mo_ref)

    @pl.when(tile % tiles_per_seq == 0)
    def _():
        c_ref[...] = jnp.zeros_like(c_ref)
        n_ref[...] = jnp.zeros_like(n_ref)
        m_ref[...] = jnp.zeros_like(m_ref)
        ext_ref[0:SUBLANES, :] = jnp.zeros((SUBLANES, ext_ref.shape[1]), F32)

    na = a_ref.shape[1]
    nm = mo_ref.shape[2]
    n_chunks = TM // MLSTM_L
    n_pieces = 4 * n_chunks
    width = o_ref.shape[1] // n_pieces

    def project(piece):
        cols = slice(piece * width, (piece + 1) * width)
        acc = _dot(a_ref[...], w_ref[0:na, cols])
        acc += _dot(mo_ref[(s + 1) % 2], w_ref[na:na + nm, cols])
        acc += _dot(cd_ref[...], w_ref[na + nm:, cols])
        o_ref[:, cols] = h_ref[:, cols] + acc

    slot = s % 2
    for k in range(n_chunks):
        rows = slice(k * MLSTM_L, (k + 1) * MLSTM_L)

        def emit(cols, value, rows=rows):
            mo_ref[slot, rows, cols] = value.astype(mo_ref.dtype)

        fillers = [functools.partial(project, 4 * k + i) for i in range(4)]
        _mlstm_chunk(rows, qk_ref, v_ref, og_ref, gates_t_ref, gbias_ref, gbias_t_ref, cw_ref, cb_ref, ghead_ref,
                     tril_ref, triu_ref, shift_ref, emit, fillers, ext_ref, qk_act_ref, c_ref, n_ref, m_ref)


def _mixout(h, z, out_a, out_c, w_out, gates_t, gbias, gbias_t, conv_w, conv_b, g_head, tril, triu, shifts,
            layer, seq):
    m = h.shape[0]
    L = MLSTM_L
    nt = m // TM
    nqk = 2 * H_MLSTM * DK_MLSTM
    nv = H_MLSTM * DV_MLSTM
    na, ncol = out_a.shape[1], out_c.shape[1]
    qk_blk = (2 * D_ATT) // nqk
    v_blk = (2 * D_ATT + nqk) // nv
    og_blk = v_blk + 1
    prev = lambda s: (jnp.maximum(s - 1, 0), 0)
    cur = lambda s: jnp.minimum(s, nt - 1)
    const = lambda s: (0, 0)
    return pl.pallas_call(
        functools.partial(_mixout_kernel, n_tiles=nt, tiles_per_seq=seq // TM),
        grid=(nt + 1,),
        in_specs=[
            pl.BlockSpec((TM, D_MODEL), prev),
            pl.BlockSpec((TM, na), prev),
            pl.BlockSpec((TM, ncol), prev),
            pl.BlockSpec((na + nv + ncol, D_MODEL), const),
            pl.BlockSpec((TM, nqk), lambda s: (cur(s), qk_blk)),
            pl.BlockSpec((TM, nv), lambda s: (cur(s), v_blk)),
            pl.BlockSpec((TM, nv), lambda s: (cur(s), og_blk)),
            pl.BlockSpec((LANES, TM), lambda s: (0, cur(s))),
            pl.BlockSpec((None, 1, LANES), lambda s: (layer, 0, 0)),
            pl.BlockSpec((None, N_GATES, 1), lambda s: (layer, 0, 0)),
            pl.BlockSpec((None, CONV_WIDTH, nqk), lambda s: (layer, 0, 0)),
            pl.BlockSpec((None, 1, nqk), lambda s: (layer, 0, 0)),
            pl.BlockSpec((None, 1, nv), lambda s: (layer, 0, 0)),
            pl.BlockSpec((L, L), const),
            pl.BlockSpec((L, L), const),
            pl.BlockSpec(((CONV_WIDTH - 1) * L, L), const),
        ],
        out_specs=pl.BlockSpec((TM, D_MODEL), prev),
        out_shape=jax.ShapeDtypeStruct((m, D_MODEL), F32),
        scratch_shapes=[
            pltpu.VMEM((2, TM, nv), BF16),
            pltpu.VMEM((2 * SUBLANES, nqk), F32),
            pltpu.VMEM((L, nqk), F32),
            pltpu.VMEM((H_MLSTM, DK_MLSTM, DV_MLSTM), F32),
            pltpu.VMEM((H_MLSTM, 1, DK_MLSTM), F32),
            pltpu.VMEM((H_MLSTM, 1, 1), F32),
        ],
        compiler_params=_cparams(("arbitrary",)),
        name="mixout",
    )(h, out_a, out_c, w_out, z, z, z, gates_t, gbias, gbias_t, conv_w, conv_b, g_head, tril, triu, shifts)


def _ple_kernel(h_ref, g_ref, p_ref, wg_ref, wp_ref, gfin_ref, *refs, final):
    n_cast = len(refs) // 2
    o_ref = refs[n_cast]
    x = h_ref[...]
    xn = _rms(x, g_ref[...]).astype(BF16)
    gate = jax.nn.sigmoid(_dot(xn, wg_ref[...]))
    proj = _dot(p_ref[...].astype(BF16), wp_ref[...])
    y = x + gate * proj
    if final:
        y = _rms(y, gfin_ref[...])
    o_ref[...] = y
    _cast_blocks(refs[:n_cast], refs[n_cast + 1:])


def _ple(h, g, p2, w_gate, w_proj, g_final, layer, next_w_in_t=None):
    m = h.shape[0]
    nm = m // TM
    in_specs = [
        pl.BlockSpec((TM, D_MODEL), lambda i: (i, 0)),
        pl.BlockSpec((None, 1, D_MODEL), lambda i: (layer, 0, 0)),
        pl.BlockSpec((None, TM, D_PLE), lambda i: (layer, i, 0)),
        pl.BlockSpec((D_MODEL, D_MODEL), lambda i: (0, 0)),
        pl.BlockSpec((D_PLE, D_MODEL), lambda i: (0, 0)),
        pl.BlockSpec((1, D_MODEL), lambda i: (0, 0)),
    ]
    out_specs = [pl.BlockSpec((TM, D_MODEL), lambda i: (i, 0))]
    out_shape = [jax.ShapeDtypeStruct((m, D_MODEL), F32)]
    inputs = [h, g, p2, w_gate, w_proj, g_final]
    if next_w_in_t is not None:
        src, nl = next_w_in_t
        blk = (D_Z // nm, D_MODEL)
        in_specs.append(pl.BlockSpec((None, *blk), lambda i: (nl, i, 0)))
        out_specs.append(pl.BlockSpec(blk, lambda i: (i, 0)))
        out_shape.append(jax.ShapeDtypeStruct((D_Z, D_MODEL), BF16))
        inputs.append(src)
    outs = pl.pallas_call(
        functools.partial(_ple_kernel, final=next_w_in_t is None),
        grid=(nm,),
        in_specs=in_specs,
        out_specs=out_specs,
        out_shape=out_shape,
        compiler_params=_cparams(("parallel",)),
        name="ple",
    )(*inputs)
    return outs[0], (outs[1] if len(outs) > 1 else None)


def _cast_kernel(src_ref, dst_ref):
    dst_ref[...] = src_ref[...].astype(BF16)


def _cast_rows(src, layer, rows):
    cols = src.shape[2]
    return pl.pallas_call(
        _cast_kernel,
        grid=(rows // TM,),
        in_specs=[pl.BlockSpec((None, TM, cols), lambda i: (layer, i, 0))],
        out_specs=pl.BlockSpec((TM, cols), lambda i: (i, 0)),
        out_shape=jax.ShapeDtypeStruct((rows, cols), BF16),
        compiler_params=_cparams(("parallel",)),
        name="cast_rows",
    )(src)


def kernel(x, p, g_ffn1, w_up1, w_down1, g_mix, w_in, conv_w, conv_b, b_igate, b_fgate, g_head, w_out, g_ffn2, w_up2, w_down2, g_ple, w_ple_gate, w_ple_proj, g_final):
    b, t, d = x.shape
    depth = p.shape[0]
    m = b * t

    ffn_w = (w_up1[0].astype(BF16), w_down1[0].astype(BF16))
    w_in_t = jnp.swapaxes(w_in, 1, 2)
    w_in_tb = _cast_rows(w_in_t, 0, D_Z)
    w_gate_t = jnp.pad(w_in_t[:, D_Z:, :], ((0, 0), (0, LANES - N_GATES), (0, 0)))
    gate_bias = jnp.concatenate([b_igate, b_fgate], axis=-1)
    gbias_col = jnp.pad(gate_bias, ((0, 0), (0, LANES - N_GATES)))[:, None, :]
    gbias_row = gate_bias[:, :, None]
    row = lambda a: a[:, None, :]
    g_ffn1r, g_mixr, g_ffn2r, g_pler = row(g_ffn1), row(g_mix), row(g_ffn2), row(g_ple)
    conv_br, g_headr = row(conv_b), row(g_head)
    g_finalr = g_final[None, :]
    p2 = p.reshape(depth, m, D_PLE)

    slopes = 2.0 ** (-8.0 * jnp.arange(1, N_ALIBI + 1, dtype=F32) / N_ALIBI)
    slopes_moba, slopes_dil = slopes[0::2], slopes[1::2]
    causal_mask = jnp.asarray(_causal_mask(t))
    log_counts = jnp.asarray(_dilated_window_log_counts(t))
    tri = np.tril(np.ones((MLSTM_L, MLSTM_L), np.float32))
    tril, triu = jnp.asarray(tri, BF16), jnp.asarray(tri.T, BF16)
    shifts = jnp.asarray(_conv_shift_matrices(MLSTM_L), BF16)

    h = x.reshape(m, d)
    for i in range(depth):
        h, ffn_w = _ffn(h, g_ffn1r, *ffn_w, i, next_weights=(w_up2, w_down2, i))
        z, gates_t, (w_outb, w_pgb, w_ppb) = _inproj(
            h, g_mixr, w_in_tb, w_gate_t, i, (w_out, w_ple_gate, w_ple_proj))
        z3 = z.reshape(b, t, D_Z)
        out_a = _attention(z3, slopes_moba, causal_mask, 0, MOBA_TOPK, "moba")
        out_c = _attention(z3, slopes_dil, log_counts, D_ATT, None, "dilated")
        h = _mixout(h, z, out_a.reshape(m, -1), out_c.reshape(m, -1), w_outb, gates_t, gbias_col, gbias_row,
                    conv_w, conv_br, g_headr, tril, triu, shifts, i, t)
        upcoming = (w_up1, w_down1, i + 1) if i + 1 < depth else None
        h, ffn_w = _ffn(h, g_ffn2r, *ffn_w, i, next_weights=upcoming)
        h, w_in_tb = _ple(h, g_pler, p2, w_pgb, w_ppb, g_finalr, i,
                          next_w_in_t=(w_in_t, i + 1) if i + 1 < depth else None)
    return h.reshape(b, t, d)
```

```python
import functools

import numpy as np
import jax
import jax.numpy as jnp
from jax import lax
from jax.experimental import pallas as pl
from jax.experimental.pallas import tpu as pltpu

F32 = jnp.float32
BF16 = jnp.bfloat16

D_MODEL = 2048
DEPTH = 4
HEAD_DIM = 128
H_MOBA = 4
H_MLSTM = 4
H_DIL = 4
DK_MLSTM = 128
DV_MLSTM = 256
D_FF = 5632
D_PLE = 256
MOBA_BLOCK = 256
MOBA_TOPK = 3
DIL_PAIRS = ((128, 1), (512, 4), (2048, 16))
CONV_WIDTH = 4
RMS_EPS = 1e-6
LOG2_E = 1.4426950408889634
NEG_INF = -1e30
N_ALIBI = H_MOBA + H_DIL
D_ATT = 3 * H_MOBA * HEAD_DIM
D_Z = 2 * D_ATT + 2 * H_MLSTM * DK_MLSTM + 2 * H_MLSTM * DV_MLSTM
N_GATES = 2 * H_MLSTM

LANES = 128
SUBLANES = 8
VMEM_LIMIT_BYTES = 60 * 1024 * 1024

TM = 512
TM_BIG = 1024
TF = 512
FFN_RESIDUAL_PIECES = 8
TN_IN = 1536
ATT_BLOCK = 256
MLSTM_L = 256


def _cparams(semantics):
    return pltpu.CompilerParams(dimension_semantics=semantics, vmem_limit_bytes=VMEM_LIMIT_BYTES)


def _rms(x, g):
    ms = jnp.mean(x * x, axis=-1, keepdims=True)
    return x * lax.rsqrt(ms + RMS_EPS) * g


def _dot(a, b):
    return jnp.dot(a, b, preferred_element_type=F32)


def _dot_nt(a, b):
    return lax.dot_general(a, b, (((1,), (1,)), ((), ())), preferred_element_type=F32)


def _cast_blocks(src_refs, dst_refs):
    for src, dst in zip(src_refs, dst_refs):
        dst[...] = src[...].astype(BF16)


def _ffn_kernel(*refs, n_cast):
    x_ref, g_ref, wg_ref, wu_ref, wd_ref = refs[:5]
    o_ref = refs[5 + n_cast]
    xn_ref = refs[-1]
    f = pl.program_id(1)
    last = pl.num_programs(1) - 1

    def step(first=False, final=False):
        if first:
            xn_ref[...] = _rms(x_ref[...], g_ref[...]).astype(BF16)
        xn = xn_ref[...]
        gate = _dot(xn, wg_ref[...])
        up = _dot(xn, wu_ref[...])
        act = (gate * jax.nn.sigmoid(gate) * up).astype(BF16)
        if first:
            o_ref[...] = _dot(act, wd_ref[...])
        elif final:
            width = o_ref.shape[1] // FFN_RESIDUAL_PIECES
            for j in range(FFN_RESIDUAL_PIECES):
                cols = slice(j * width, (j + 1) * width)
                total = o_ref[:, cols] + _dot(act, wd_ref[:, cols])
                o_ref[:, cols] = x_ref[:, cols] + 0.5 * total
        else:
            o_ref[...] += _dot(act, wd_ref[...])
        _cast_blocks(refs[5:5 + n_cast], refs[6 + n_cast:6 + 2 * n_cast])

    pl.when(f == 0)(functools.partial(step, first=True))
    pl.when((f > 0) & (f < last))(step)
    pl.when(f == last)(functools.partial(step, final=True))


def _ffn(h, g, w_up, w_down, layer, next_up=None):
    m = h.shape[0]
    nm, nf = m // TM_BIG, D_FF // TF
    in_specs = [
        pl.BlockSpec((TM_BIG, D_MODEL), lambda i, f: (i, 0)),
        pl.BlockSpec((None, 1, D_MODEL), lambda i, f: (layer, 0, 0)),
        pl.BlockSpec((D_MODEL, TF), lambda i, f: (0, f)),
        pl.BlockSpec((D_MODEL, TF), lambda i, f: (0, f + nf)),
        pl.BlockSpec((TF, D_MODEL), lambda i, f: (f, 0)),
    ]
    out_specs = [pl.BlockSpec((TM_BIG, D_MODEL), lambda i, f: (i, 0))]
    out_shape = [jax.ShapeDtypeStruct((m, D_MODEL), F32)]
    inputs = [h, g, w_up, w_up, w_down]
    if next_up is not None:
        src_up, nl = next_up
        up_blk = (D_MODEL // nm, 2 * D_FF // nf)
        in_specs.append(pl.BlockSpec((None, *up_blk), lambda i, f: (nl, i, f)))
        out_specs.append(pl.BlockSpec(up_blk, lambda i, f: (i, f)))
        out_shape.append(jax.ShapeDtypeStruct(src_up.shape[1:], BF16))
        inputs.append(src_up)
    outs = pl.pallas_call(
        functools.partial(_ffn_kernel, n_cast=len(inputs) - 5),
        grid=(nm, nf),
        in_specs=in_specs,
        out_specs=out_specs,
        out_shape=out_shape,
        scratch_shapes=[pltpu.VMEM((TM_BIG, D_MODEL), BF16)],
        compiler_params=_cparams(("parallel", "arbitrary")),
        name="ffn",
    )(*inputs)
    return outs[0], (outs[1] if len(outs) > 1 else None)


def _inproj_kernel(x_ref, g_ref, w_ref, wgate_t_ref, *refs):
    n_cast = (len(refs) - 3) // 2
    z_ref, gates_t_ref = refs[n_cast:n_cast + 2]
    xn_ref = refs[-1]
    n = pl.program_id(1)

    def step(first=False):
        if first:
            xn_ref[...] = _rms(x_ref[...], g_ref[...]).astype(BF16)
        xn = xn_ref[...]
        z_ref[...] = _dot_nt(xn, w_ref[...]).astype(z_ref.dtype)
        if first:
            gates_t_ref[...] = _dot_nt(wgate_t_ref[...].astype(BF16), xn)
        _cast_blocks(refs[:n_cast], refs[n_cast + 2:2 * n_cast + 2])

    pl.when(n == 0)(functools.partial(step, first=True))
    pl.when(n > 0)(step)


def _inproj(h, g, w_in_t, w_gate_t, layer, later_weights):
    m = h.shape[0]
    nm, nn = m // TM_BIG, D_Z // TN_IN
    in_specs = [
        pl.BlockSpec((TM_BIG, D_MODEL), lambda i, n: (i, 0)),
        pl.BlockSpec((None, 1, D_MODEL), lambda i, n: (layer, 0, 0)),
        pl.BlockSpec((TN_IN, D_MODEL), lambda i, n: (n, 0)),
        pl.BlockSpec((None, LANES, D_MODEL), lambda i, n: (layer, 0, 0)),
    ]
    out_specs = [
        pl.BlockSpec((TM_BIG, TN_IN), lambda i, n: (i, n)),
        pl.BlockSpec((LANES, TM_BIG), lambda i, n: (0, i)),
    ]
    out_shape = [
        jax.ShapeDtypeStruct((m, D_Z), BF16),
        jax.ShapeDtypeStruct((LANES, m), F32),
    ]
    for w in later_weights:
        blk = (w.shape[1] // nm, w.shape[2] // nn)
        in_specs.append(pl.BlockSpec((None, *blk), lambda i, n: (layer, i, n)))
        out_specs.append(pl.BlockSpec(blk, lambda i, n: (i, n)))
        out_shape.append(jax.ShapeDtypeStruct(w.shape[1:], BF16))
    outs = pl.pallas_call(
        _inproj_kernel,
        grid=(nm, nn),
        in_specs=in_specs,
        out_specs=out_specs,
        out_shape=out_shape,
        scratch_shapes=[pltpu.VMEM((TM_BIG, D_MODEL), BF16)],
        compiler_params=_cparams(("parallel", "arbitrary")),
        name="inproj",
    )(h, g, w_in_t, w_gate_t, *later_weights)
    return outs[0], outs[1], tuple(outs[2:])


def _head_cols(h):
    return slice(h * HEAD_DIM, (h + 1) * HEAD_DIM)


def _stage_values(v_ref, vt_ref, nblk):
    for h in range(vt_ref.shape[0]):
        for j in range(nblk):
            vj = v_ref[j * ATT_BLOCK:(j + 1) * ATT_BLOCK, _head_cols(h)].astype(F32)
            vt_ref[h, :, j * ATT_BLOCK:(j + 1) * ATT_BLOCK] = vj.T.astype(BF16)


def _local_distance():
    qi = lax.broadcasted_iota(jnp.int32, (ATT_BLOCK, ATT_BLOCK), 1)
    ki = lax.broadcasted_iota(jnp.int32, (ATT_BLOCK, ATT_BLOCK), 0)
    return (qi - ki).astype(F32)


def _fold_sublanes(x, op):
    return op(x.reshape(ATT_BLOCK // SUBLANES, SUBLANES, x.shape[-1]), axis=0)


def _softmax_pv(cc, score_fns, s_ref, p_ref, vt_ref, o_ref):
    nk = (cc + 1) * ATT_BLOCK
    blocks = [slice(j * ATT_BLOCK, (j + 1) * ATT_BLOCK) for j in range(cc + 1)]
    maxima = []
    for h, score_fn in enumerate(score_fns):
        m8 = None
        for j, blk in enumerate(blocks):
            s = score_fn(j)
            s_ref[h, blk, :] = s
            part = _fold_sublanes(s, jnp.max)
            m8 = part if m8 is None else jnp.maximum(m8, part)
        maxima.append(jnp.max(m8, axis=0, keepdims=True))
    for h, m in enumerate(maxima):
        l8 = None
        for blk in blocks:
            p = jnp.exp2(s_ref[h, blk, :] - m)
            p_ref[h, blk, :] = p.astype(BF16)
            part = _fold_sublanes(p, jnp.sum)
            l8 = part if l8 is None else l8 + part
        l = jnp.sum(l8, axis=0, keepdims=True)
        acc = _dot(vt_ref[h, :, 0:nk], p_ref[h, 0:nk, :])
        o_ref[:, h * HEAD_DIM:(h + 1) * HEAD_DIM] = (acc / l).T.astype(o_ref.dtype)


def _attn_kernel(slope_ref, q_ref, k_ref, v_ref, mask_ref, o_ref, vt_ref, s_ref, p_ref, bias_ref, kmean_ref, *,
                 nblk, topk):
    c = pl.program_id(1)
    nh = vt_ref.shape[0]
    qk_scale = HEAD_DIM ** -0.5 * LOG2_E

    def stage():
        _stage_values(v_ref, vt_ref, nblk)
        dist = _local_distance()
        for h in range(nh):
            slope2 = slope_ref[h] * LOG2_E
            for diff in range(nblk):
                bias_ref[h, diff] = mask_ref[diff] - slope2 * (dist + float(diff * ATT_BLOCK))
            if topk is not None:
                for j in range(nblk):
                    kj = k_ref[j * ATT_BLOCK:(j + 1) * ATT_BLOCK, _head_cols(h)].astype(F32)
                    kmean_ref[h, j:j + 1, :] = jnp.mean(kj, axis=0, keepdims=True)

    def head_scores(cc, h):
        q = q_ref[:, _head_cols(h)]
        if topk is not None and cc > topk:
            km = kmean_ref[h]
            km_hi = km.astype(BF16)
            km_lo = (km - km_hi.astype(F32)).astype(BF16)
            gate = _dot_nt(km_hi, q) + _dot_nt(km_lo, q)
            g = [gate[n:n + 1, :] for n in range(cc)]
            sel = []
            for n in range(cc):
                n_ahead = jnp.zeros_like(g[n])
                for mth in range(cc):
                    if mth != n:
                        ahead = (g[mth] >= g[n]) if mth < n else (g[mth] > g[n])
                        n_ahead = n_ahead + jnp.where(ahead, 1.0, 0.0)
                sel.append(jnp.where(n_ahead < topk, 0.0, NEG_INF))
        else:
            sel = None

        def score(j):
            kj = k_ref[j * ATT_BLOCK:(j + 1) * ATT_BLOCK, _head_cols(h)]
            s = _dot_nt(kj, q) * qk_scale + bias_ref[h, cc - j]
            return s if (sel is None or j == cc) else s + sel[j]

        return score

    def branch(cc):
        if cc == 0:
            stage()
        _softmax_pv(cc, [head_scores(cc, h) for h in range(nh)], s_ref, p_ref, vt_ref, o_ref)

    for cc in range(nblk):
        pl.when(c == cc)(functools.partial(branch, cc))


def _block_distances(seq):
    nblk = seq // ATT_BLOCK
    kl = np.arange(ATT_BLOCK)[None, :, None]
    ql = np.arange(ATT_BLOCK)[None, None, :]
    return np.arange(nblk)[:, None, None] * ATT_BLOCK + ql - kl


def _causal_mask(seq):
    return np.where(_block_distances(seq) >= 0, 0.0, NEG_INF).astype(np.float32)


def _dilated_window_log_counts(seq):
    d = _block_distances(seq)
    cnt = np.zeros(d.shape, np.float64)
    for window, dil in DIL_PAIRS:
        cnt += (d >= 0) & (d <= window) & (d % dil == 0)
    return np.where(cnt > 0, np.log2(np.maximum(cnt, 1.0)), NEG_INF).astype(np.float32)


def _attention(z3, slopes, mask, col0, topk, name):
    b, t, _ = z3.shape
    nblk = t // ATT_BLOCK
    nh = slopes.shape[0]
    width = nh * HEAD_DIM
    cb = col0 // width
    return pl.pallas_call(
        functools.partial(_attn_kernel, nblk=nblk, topk=topk),
        grid=(b, nblk),
        in_specs=[
            pl.BlockSpec(memory_space=pltpu.SMEM),
            pl.BlockSpec((None, ATT_BLOCK, width), lambda i, c: (i, c, cb)),
            pl.BlockSpec((None, t, width), lambda i, c: (i, 0, cb + 1)),
            pl.BlockSpec((None, t, width), lambda i, c: (i, 0, cb + 2)),
            pl.BlockSpec((nblk, ATT_BLOCK, ATT_BLOCK), lambda i, c: (0, 0, 0)),
        ],
        out_specs=pl.BlockSpec((None, ATT_BLOCK, width), lambda i, c: (i, c, 0)),
        out_shape=jax.ShapeDtypeStruct((b, t, width), BF16),
        scratch_shapes=[
            pltpu.VMEM((nh, HEAD_DIM, t), BF16),
            pltpu.VMEM((nh, t, ATT_BLOCK), F32),
            pltpu.VMEM((nh, t, ATT_BLOCK), BF16),
            pltpu.VMEM((nh, nblk, ATT_BLOCK, ATT_BLOCK), F32),
            pltpu.VMEM((nh, nblk, HEAD_DIM), F32),
        ],
        compiler_params=_cparams(("parallel", "arbitrary")),
        name=name,
    )(slopes, z3, z3, z3, mask)


def _log_sigmoid(x):
    return jnp.minimum(x, 0.0) - jnp.log(1.0 + jnp.exp(-jnp.abs(x)))


def _split3(x):
    hi = x.astype(BF16)
    r1 = x - hi.astype(F32)
    mid = r1.astype(BF16)
    lo = (r1 - mid.astype(F32)).astype(BF16)
    return hi, mid, lo


def _mlstm_chunk(rows, qk_ref, v_ref, og_ref, gates_t_ref, gbias_ref, gbias_t_ref, cw_ref, cb_ref,
                 ghead_ref, tril_ref, triu_ref, shift_ref, emit, fillers, ext_ref, qk_act_ref, c_ref, n_ref, m_ref):
    L = MLSTM_L
    nqk = H_MLSTM * DK_MLSTM
    fill = iter(fillers)
    next(fill)()

    u_bf = qk_ref[rows, :]
    u = u_bf.astype(F32)
    shifted = _dot(shift_ref[...], u_bf)
    cw = cw_ref[...]
    w_rows = [cw[CONV_WIDTH - 1 - back:CONV_WIDTH - back, :] for back in range(CONV_WIDTH)]
    y = cb_ref[...] + w_rows[0] * u
    ext_ref[SUBLANES:2 * SUBLANES, :] = u[0:SUBLANES, :]
    y_head = cb_ref[...] + w_rows[0] * u[0:SUBLANES, :]
    for back in range(1, CONV_WIDTH):
        y = y + w_rows[back] * shifted[(back - 1) * L:back * L, :]
        y_head = y_head + w_rows[back] * ext_ref[SUBLANES - back:2 * SUBLANES - back, :]
    ext_ref[0:SUBLANES, :] = u[L - SUBLANES:L, :]
    qk_act_ref[...] = y * jax.nn.sigmoid(y)
    qk_act_ref[0:SUBLANES, :] = y_head * jax.nn.sigmoid(y_head)

    next(fill)()
    gates_t = gates_t_ref[:, rows]
    g_row = gates_t[0:N_GATES, :] + gbias_t_ref[...]
    g_col = gates_t.T + gbias_ref[...]
    lf_col = _log_sigmoid(g_col)
    lf_row = _log_sigmoid(g_row)
    b_col_all = sum(_dot(tril_ref[...], t) for t in _split3(lf_col))
    b_row_all = sum(_dot(t, triu_ref[...]) for t in _split3(lf_row))

    ti = lax.broadcasted_iota(jnp.int32, (L, L), 0)
    si = lax.broadcasted_iota(jnp.int32, (L, L), 1)
    causal = si <= ti

    heads = []
    for h in range(H_MLSTM):
        q = qk_act_ref[:, h * DK_MLSTM:(h + 1) * DK_MLSTM]
        k = qk_act_ref[:, nqk + h * DK_MLSTM:nqk + (h + 1) * DK_MLSTM] * (DK_MLSTM ** -0.5)
        qb = q.astype(BF16)
        li_col = g_col[:, h:h + 1]
        li_row = g_row[h:h + 1, :]
        b_col = b_col_all[:, H_MLSTM + h:H_MLSTM + h + 1]
        b_row = b_row_all[H_MLSTM + h:H_MLSTM + h + 1, :]
        m_st = m_ref[h]
        d_log = jnp.where(causal, b_col - b_row + li_row, NEG_INF)
        m_inter = b_col + m_st
        m_t = jnp.maximum(m_inter, jnp.max(d_log, axis=-1, keepdims=True))
        s = _dot_nt(qb, k.astype(BF16)) * jnp.exp(d_log - m_t)
        heads.append(dict(q=q, k=k, qb=qb, s=s, m_t=m_t, w_inter=jnp.exp(m_inter - m_t), m_st=m_st,
                          c_st=c_ref[h], n_st=n_ref[h],
                          li_col=li_col, li_row=li_row, b_col=b_col, b_row=b_row))

    next(fill)()
    for h, hd in enumerate(heads):
        v = v_ref[rows, h * DV_MLSTM:(h + 1) * DV_MLSTM]
        b_last = hd["b_col"][L - 1:L, :]
        g_dec_row = b_last - hd["b_row"] + hd["li_row"]
        g_dec_col = b_last - hd["b_col"] + hd["li_col"]
        m_new = jnp.maximum(b_last + hd["m_st"], jnp.max(g_dec_row, axis=-1, keepdims=True))
        w_old = jnp.exp(b_last + hd["m_st"] - m_new)
        kw = hd["k"] * jnp.exp(g_dec_col - m_new)
        c_ref[h] = w_old * hd["c_st"] + _dot(kw.T.astype(BF16), v)
        n_ref[h] = w_old * hd["n_st"] + jnp.sum(kw, axis=0, keepdims=True)
        m_ref[h] = m_new

    next(fill)()
    for h, hd in enumerate(heads):
        cols = slice(h * DV_MLSTM, (h + 1) * DV_MLSTM)
        v = v_ref[rows, cols]
        num = hd["w_inter"] * _dot(hd["qb"], hd["c_st"].astype(BF16)) + _dot(hd["s"].astype(BF16), v)
        den = (hd["w_inter"] * jnp.sum(hd["q"] * hd["n_st"], axis=-1, keepdims=True)
               + jnp.sum(hd["s"], axis=-1, keepdims=True))
        hh = num / jnp.maximum(jnp.abs(den), jnp.exp(-hd["m_t"]))
        hh = hh * lax.rsqrt(jnp.mean(hh * hh, axis=-1, keepdims=True) + RMS_EPS)
        hh = hh * ghead_ref[:, cols]
        emit(cols, jax.nn.sigmoid(og_ref[rows, cols].astype(F32)) * hh)


def _conv_shift_matrices(length):
    t = np.arange(length)
    mats = [(t[:, None] - t[None, :] == s).astype(np.float32) for s in range(1, CONV_WIDTH)]
    return np.concatenate(mats, axis=0)


def _mixout_kernel(h_ref, a_ref, cd_ref, w_ref, qk_ref, v_ref, og_ref, gates_t_ref, gbias_ref, gbias_t_ref, cw_ref,
                   cb_ref, ghead_ref, tril_ref, triu_ref, shift_ref, wsrc_ref, o_ref, wdst_ref, mo_ref, ext_ref,
                   qk_act_ref, c_ref, n_ref, m_ref, *, n_tiles, tiles_per_seq):
    s = pl.program_id(0)
    tile = jnp.minimum(s, n_tiles - 1)

    @pl.when(s == 0)
    def _():
        mo_ref[...] = jnp.zeros_like(mo_ref)

    @pl.when(tile % tiles_per_seq == 0)
    def _():
        c_ref[...] = jnp.zeros_like(c_ref)
        n_ref[...] = jnp.zeros_like(n_ref)
        m_ref[...] = jnp.zeros_like(m_ref)
        ext_ref[0:SUBLANES, :] = jnp.zeros((SUBLANES, ext_ref.shape[1]), F32)

    na = a_ref.shape[1]
    nm = mo_ref.shape[2]
    n_chunks = TM // MLSTM_L
    n_pieces = 4 * n_chunks
    width = o_ref.shape[1] // n_pieces

    def project(piece):
        cols = slice(piece * width, (piece + 1) * width)
        acc = _dot(a_ref[...], w_ref[0:na, cols])
        acc += _dot(mo_ref[(s + 1) % 2], w_ref[na:na + nm, cols])
        acc += _dot(cd_ref[...], w_ref[na + nm:, cols])
        o_ref[:, cols] = h_ref[:, cols] + acc

    _cast_blocks([wsrc_ref], [wdst_ref])
    slot = s % 2
    for k in range(n_chunks):
        rows = slice(k * MLSTM_L, (k + 1) * MLSTM_L)

        def emit(cols, value, rows=rows):
            mo_ref[slot, rows, cols] = value.astype(mo_ref.dtype)

        fillers = [functools.partial(project, 4 * k + i) for i in range(4)]
        _mlstm_chunk(rows, qk_ref, v_ref, og_ref, gates_t_ref, gbias_ref, gbias_t_ref, cw_ref, cb_ref, ghead_ref,
                     tril_ref, triu_ref, shift_ref, emit, fillers, ext_ref, qk_act_ref, c_ref, n_ref, m_ref)


def _mixout(h, z, out_a, out_c, w_out, gates_t, gbias, gbias_t, conv_w, conv_b, g_head, tril, triu, shifts,
            layer, seq, cast_src):
    m = h.shape[0]
    L = MLSTM_L
    nt = m // TM
    nqk = 2 * H_MLSTM * DK_MLSTM
    nv = H_MLSTM * DV_MLSTM
    na, ncol = out_a.shape[1], out_c.shape[1]
    qk_blk = (2 * D_ATT) // nqk
    v_blk = (2 * D_ATT + nqk) // nv
    og_blk = v_blk + 1
    prev = lambda s: (jnp.maximum(s - 1, 0), 0)
    cur = lambda s: jnp.minimum(s, nt - 1)
    const = lambda s: (0, 0)
    cast_blk = (cast_src.shape[1] // nt, cast_src.shape[2])
    outs = pl.pallas_call(
        functools.partial(_mixout_kernel, n_tiles=nt, tiles_per_seq=seq // TM),
        grid=(nt + 1,),
        in_specs=[
            pl.BlockSpec((TM, D_MODEL), prev),
            pl.BlockSpec((TM, na), prev),
            pl.BlockSpec((TM, ncol), prev),
            pl.BlockSpec((na + nv + ncol, D_MODEL), const),
            pl.BlockSpec((TM, nqk), lambda s: (cur(s), qk_blk)),
            pl.BlockSpec((TM, nv), lambda s: (cur(s), v_blk)),
            pl.BlockSpec((TM, nv), lambda s: (cur(s), og_blk)),
            pl.BlockSpec((LANES, TM), lambda s: (0, cur(s))),
            pl.BlockSpec((None, 1, LANES), lambda s: (layer, 0, 0)),
            pl.BlockSpec((None, N_GATES, 1), lambda s: (layer, 0, 0)),
            pl.BlockSpec((None, CONV_WIDTH, nqk), lambda s: (layer, 0, 0)),
            pl.BlockSpec((None, 1, nqk), lambda s: (layer, 0, 0)),
            pl.BlockSpec((None, 1, nv), lambda s: (layer, 0, 0)),
            pl.BlockSpec((L, L), const),
            pl.BlockSpec((L, L), const),
            pl.BlockSpec(((CONV_WIDTH - 1) * L, L), const),
            pl.BlockSpec((None, *cast_blk), lambda s: (layer, cur(s), 0)),
        ],
        out_specs=[pl.BlockSpec((TM, D_MODEL), prev), pl.BlockSpec(cast_blk, lambda s: (cur(s), 0))],
        out_shape=[jax.ShapeDtypeStruct((m, D_MODEL), F32), jax.ShapeDtypeStruct(cast_src.shape[1:], BF16)],
        scratch_shapes=[
            pltpu.VMEM((2, TM, nv), BF16),
            pltpu.VMEM((2 * SUBLANES, nqk), F32),
            pltpu.VMEM((L, nqk), F32),
            pltpu.VMEM((H_MLSTM, DK_MLSTM, DV_MLSTM), F32),
            pltpu.VMEM((H_MLSTM, 1, DK_MLSTM), F32),
            pltpu.VMEM((H_MLSTM, 1, 1), F32),
        ],
        compiler_params=_cparams(("arbitrary",)),
        name="mixout",
    )(h, out_a, out_c, w_out, z, z, z, gates_t, gbias, gbias_t, conv_w, conv_b, g_head, tril, triu, shifts, cast_src)
    return outs[0], outs[1]


def _ple_kernel(h_ref, g_ref, p_ref, wg_ref, wp_ref, gfin_ref, *refs, final):
    n_cast = len(refs) // 2
    o_ref = refs[n_cast]
    x = h_ref[...]
    xn = _rms(x, g_ref[...]).astype(BF16)
    gate = jax.nn.sigmoid(_dot(xn, wg_ref[...]))
    proj = _dot(p_ref[...].astype(BF16), wp_ref[...])
    y = x + gate * proj
    if final:
        y = _rms(y, gfin_ref[...])
    o_ref[...] = y
    _cast_blocks(refs[:n_cast], refs[n_cast + 1:])


def _ple(h, g, p2, w_gate, w_proj, g_final, layer, final, casts=()):
    m = h.shape[0]
    nm = m // TM
    in_specs = [
        pl.BlockSpec((TM, D_MODEL), lambda i: (i, 0)),
        pl.BlockSpec((None, 1, D_MODEL), lambda i: (layer, 0, 0)),
        pl.BlockSpec((None, TM, D_PLE), lambda i: (layer, i, 0)),
        pl.BlockSpec((D_MODEL, D_MODEL), lambda i: (0, 0)),
        pl.BlockSpec((D_PLE, D_MODEL), lambda i: (0, 0)),
        pl.BlockSpec((1, D_MODEL), lambda i: (0, 0)),
    ]
    out_specs = [pl.BlockSpec((TM, D_MODEL), lambda i: (i, 0))]
    out_shape = [jax.ShapeDtypeStruct((m, D_MODEL), F32)]
    inputs = [h, g, p2, w_gate, w_proj, g_final]
    for src, nl, rows in casts:
        blk = (rows // nm, src.shape[2])
        in_specs.append(pl.BlockSpec((None, *blk), lambda i, nl=nl: (nl, i, 0)))
        out_specs.append(pl.BlockSpec(blk, lambda i: (i, 0)))
        out_shape.append(jax.ShapeDtypeStruct((rows, src.shape[2]), BF16))
        inputs.append(src)
    outs = pl.pallas_call(
        functools.partial(_ple_kernel, final=final),
        grid=(nm,),
        in_specs=in_specs,
        out_specs=out_specs,
        out_shape=out_shape,
        compiler_params=_cparams(("parallel",)),
        name="ple",
    )(*inputs)
    return outs[0], tuple(outs[1:])


def _cast_kernel(src_ref, dst_ref):
    dst_ref[...] = src_ref[...].astype(BF16)


def _cast_rows(src, layer, rows):
    cols = src.shape[2]
    return pl.pallas_call(
        _cast_kernel,
        grid=(rows // TM,),
        in_specs=[pl.BlockSpec((None, TM, cols), lambda i: (layer, i, 0))],
        out_specs=pl.BlockSpec((TM, cols), lambda i: (i, 0)),
        out_shape=jax.ShapeDtypeStruct((rows, cols), BF16),
        compiler_params=_cparams(("parallel",)),
        name="cast_rows",
    )(src)


def kernel(x, p, g_ffn1, w_up1, w_down1, g_mix, w_in, conv_w, conv_b, b_igate, b_fgate, g_head, w_out, g_ffn2, w_up2, w_down2, g_ple, w_ple_gate, w_ple_proj, g_final):
    b, t, d = x.shape
    depth = p.shape[0]
    m = b * t

    w_in_t = jnp.swapaxes(w_in, 1, 2)
    w_in_tb = _cast_rows(w_in_t, 0, D_Z)
    w_gate_t = jnp.pad(w_in_t[:, D_Z:, :], ((0, 0), (0, LANES - N_GATES), (0, 0)))
    gate_bias = jnp.concatenate([b_igate, b_fgate], axis=-1)
    gbias_col = jnp.pad(gate_bias, ((0, 0), (0, LANES - N_GATES)))[:, None, :]
    gbias_row = gate_bias[:, :, None]
    row = lambda a: a[:, None, :]
    g_ffn1r, g_mixr, g_ffn2r, g_pler = row(g_ffn1), row(g_mix), row(g_ffn2), row(g_ple)
    conv_br, g_headr = row(conv_b), row(g_head)
    g_finalr = g_final[None, :]
    p2 = p.reshape(depth, m, D_PLE)

    slopes = 2.0 ** (-8.0 * jnp.arange(1, N_ALIBI + 1, dtype=F32) / N_ALIBI)
    slopes_moba, slopes_dil = slopes[0::2], slopes[1::2]
    causal_mask = jnp.asarray(_causal_mask(t))
    log_counts = jnp.asarray(_dilated_window_log_counts(t))
    tri = np.tril(np.ones((MLSTM_L, MLSTM_L), np.float32))
    tril, triu = jnp.asarray(tri, BF16), jnp.asarray(tri.T, BF16)
    shifts = jnp.asarray(_conv_shift_matrices(MLSTM_L), BF16)

    h = x.reshape(m, d)
    w_upb, w_downb = w_up1[0].astype(BF16), w_down1[0].astype(BF16)
    for i in range(depth):
        h, w_up2b = _ffn(h, g_ffn1r, w_upb, w_downb, i, next_up=(w_up2, i))
        z, gates_t, (w_outb, w_pgb, w_ppb) = _inproj(
            h, g_mixr, w_in_tb, w_gate_t, i, (w_out, w_ple_gate, w_ple_proj))
        z3 = z.reshape(b, t, D_Z)
        out_a = _attention(z3, slopes_moba, causal_mask, 0, MOBA_TOPK, "moba")
        out_c = _attention(z3, slopes_dil, log_counts, D_ATT, None, "dilated")
        h, w_down2b = _mixout(h, z, out_a.reshape(m, -1), out_c.reshape(m, -1), w_outb, gates_t, gbias_col,
                              gbias_row, conv_w, conv_br, g_headr, tril, triu, shifts, i, t, w_down2)
        last = i + 1 == depth
        h, w_upb = _ffn(h, g_ffn2r, w_up2b, w_down2b, i, next_up=None if last else (w_up1, i + 1))
        casts = () if last else ((w_in_t, i + 1, D_Z), (w_down1, i + 1, D_FF))
        h, staged = _ple(h, g_pler, p2, w_pgb, w_ppb, g_finalr, i, last, casts)
        if not last:
            w_in_tb, w_downb = staged
    return h.reshape(b, t, d)
```

```python
import functools

import numpy as np
import jax
import jax.numpy as jnp
from jax import lax
from jax.experimental import pallas as pl
from jax.experimental.pallas import tpu as pltpu

F32 = jnp.float32
BF16 = jnp.bfloat16

D_MODEL = 2048
DEPTH = 4
HEAD_DIM = 128
H_MOBA = 4
H_MLSTM = 4
H_DIL = 4
DK_MLSTM = 128
DV_MLSTM = 256
D_FF = 5632
D_PLE = 256
MOBA_BLOCK = 256
MOBA_TOPK = 3
DIL_PAIRS = ((128, 1), (512, 4), (2048, 16))
CONV_WIDTH = 4
RMS_EPS = 1e-6
LOG2_E = 1.4426950408889634
NEG_INF = -1e30
N_ALIBI = H_MOBA + H_DIL
D_ATT = 3 * H_MOBA * HEAD_DIM
D_Z = 2 * D_ATT + 2 * H_MLSTM * DK_MLSTM + 2 * H_MLSTM * DV_MLSTM
N_GATES = 2 * H_MLSTM

LANES = 128
SUBLANES = 8
VMEM_LIMIT_BYTES = 60 * 1024 * 1024

TM = 512
TM_BIG = 1024
TF = 512
FFN_RESIDUAL_PIECES = 8
TN_IN = 1536
ATT_BLOCK = 256
MLSTM_L = 256


def _cparams(semantics):
    return pltpu.CompilerParams(dimension_semantics=semantics, vmem_limit_bytes=VMEM_LIMIT_BYTES)


def _rms(x, g):
    ms = jnp.mean(x * x, axis=-1, keepdims=True)
    return x * lax.rsqrt(ms + RMS_EPS) * g


def _dot(a, b):
    return jnp.dot(a, b, preferred_element_type=F32)


def _dot_nt(a, b):
    return lax.dot_general(a, b, (((1,), (1,)), ((), ())), preferred_element_type=F32)


def _cast_blocks(src_refs, dst_refs):
    for src, dst in zip(src_refs, dst_refs):
        dst[...] = src[...].astype(BF16)


def _ffn_kernel(*refs, n_cast):
    x_ref, g_ref, wg_ref, wu_ref, wd_ref = refs[:5]
    o_ref = refs[5 + n_cast]
    xn_ref = refs[-1]
    f = pl.program_id(1)
    last = pl.num_programs(1) - 1

    def step(first=False, final=False):
        if first:
            xn_ref[...] = _rms(x_ref[...], g_ref[...]).astype(BF16)
        xn = xn_ref[...]
        gate = _dot(xn, wg_ref[...])
        up = _dot(xn, wu_ref[...])
        act = (gate * jax.nn.sigmoid(gate) * up).astype(BF16)
        if first:
            o_ref[...] = _dot(act, wd_ref[...])
        elif final:
            width = o_ref.shape[1] // FFN_RESIDUAL_PIECES
            for j in range(FFN_RESIDUAL_PIECES):
                cols = slice(j * width, (j + 1) * width)
                total = o_ref[:, cols] + _dot(act, wd_ref[:, cols])
                o_ref[:, cols] = x_ref[:, cols] + 0.5 * total
        else:
            o_ref[...] += _dot(act, wd_ref[...])
        _cast_blocks(refs[5:5 + n_cast], refs[6 + n_cast:6 + 2 * n_cast])

    pl.when(f == 0)(functools.partial(step, first=True))
    pl.when((f > 0) & (f < last))(step)
    pl.when(f == last)(functools.partial(step, final=True))


def _ffn(h, g, w_up, w_down, layer, next_weights=None):
    m = h.shape[0]
    nm, nf = m // TM_BIG, D_FF // TF
    in_specs = [
        pl.BlockSpec((TM_BIG, D_MODEL), lambda i, f: (i, 0)),
        pl.BlockSpec((None, 1, D_MODEL), lambda i, f: (layer, 0, 0)),
        pl.BlockSpec((D_MODEL, TF), lambda i, f: (0, f)),
        pl.BlockSpec((D_MODEL, TF), lambda i, f: (0, f + nf)),
        pl.BlockSpec((TF, D_MODEL), lambda i, f: (f, 0)),
    ]
    out_specs = [pl.BlockSpec((TM_BIG, D_MODEL), lambda i, f: (i, 0))]
    out_shape = [jax.ShapeDtypeStruct((m, D_MODEL), F32)]
    inputs = [h, g, w_up, w_up, w_down]
    if next_weights is not None:
        src_up, src_down, nl = next_weights
        up_blk = (D_MODEL // nm, 2 * D_FF // nf)
        down_blk = (D_FF // nf, D_MODEL // nm)
        in_specs += [
            pl.BlockSpec((None, *up_blk), lambda i, f: (nl, i, f)),
            pl.BlockSpec((None, *down_blk), lambda i, f: (nl, f, i)),
        ]
        out_specs += [pl.BlockSpec(up_blk, lambda i, f: (i, f)), pl.BlockSpec(down_blk, lambda i, f: (f, i))]
        out_shape += [jax.ShapeDtypeStruct(src_up.shape[1:], BF16), jax.ShapeDtypeStruct(src_down.shape[1:], BF16)]
        inputs += [src_up, src_down]
    outs = pl.pallas_call(
        functools.partial(_ffn_kernel, n_cast=len(inputs) - 5),
        grid=(nm, nf),
        in_specs=in_specs,
        out_specs=out_specs,
        out_shape=out_shape,
        scratch_shapes=[pltpu.VMEM((TM_BIG, D_MODEL), BF16)],
        compiler_params=_cparams(("parallel", "arbitrary")),
        name="ffn",
    )(*inputs)
    return outs[0], tuple(outs[1:])


def _inproj_kernel(x_ref, g_ref, w_ref, wgate_t_ref, *refs):
    n_cast = (len(refs) - 3) // 2
    z_ref, gates_t_ref = refs[n_cast:n_cast + 2]
    xn_ref = refs[-1]
    n = pl.program_id(1)

    def step(first=False):
        if first:
            xn_ref[...] = _rms(x_ref[...], g_ref[...]).astype(BF16)
        xn = xn_ref[...]
        z_ref[...] = _dot_nt(xn, w_ref[...]).astype(z_ref.dtype)
        if first:
            gates_t_ref[...] = _dot_nt(wgate_t_ref[...].astype(BF16), xn)
        _cast_blocks(refs[:n_cast], refs[n_cast + 2:2 * n_cast + 2])

    pl.when(n == 0)(functools.partial(step, first=True))
    pl.when(n > 0)(step)


def _inproj(h, g, w_in_t, w_gate_t, layer, later_weights):
    m = h.shape[0]
    nm, nn = m // TM_BIG, D_Z // TN_IN
    in_specs = [
        pl.BlockSpec((TM_BIG, D_MODEL), lambda i, n: (i, 0)),
        pl.BlockSpec((None, 1, D_MODEL), lambda i, n: (layer, 0, 0)),
        pl.BlockSpec((TN_IN, D_MODEL), lambda i, n: (n, 0)),
        pl.BlockSpec((None, LANES, D_MODEL), lambda i, n: (layer, 0, 0)),
    ]
    out_specs = [
        pl.BlockSpec((TM_BIG, TN_IN), lambda i, n: (i, n)),
        pl.BlockSpec((LANES, TM_BIG), lambda i, n: (0, i)),
    ]
    out_shape = [
        jax.ShapeDtypeStruct((m, D_Z), BF16),
        jax.ShapeDtypeStruct((LANES, m), F32),
    ]
    for w in later_weights:
        blk = (w.shape[1] // nm, w.shape[2] // nn)
        in_specs.append(pl.BlockSpec((None, *blk), lambda i, n: (layer, i, n)))
        out_specs.append(pl.BlockSpec(blk, lambda i, n: (i, n)))
        out_shape.append(jax.ShapeDtypeStruct(w.shape[1:], BF16))
    outs = pl.pallas_call(
        _inproj_kernel,
        grid=(nm, nn),
        in_specs=in_specs,
        out_specs=out_specs,
        out_shape=out_shape,
        scratch_shapes=[pltpu.VMEM((TM_BIG, D_MODEL), BF16)],
        compiler_params=_cparams(("parallel", "arbitrary")),
        name="inproj",
    )(h, g, w_in_t, w_gate_t, *later_weights)
    return outs[0], outs[1], tuple(outs[2:])


def _head_cols(h):
    return slice(h * HEAD_DIM, (h + 1) * HEAD_DIM)


def _stage_values(v_ref, vt_ref, nblk):
    for h in range(vt_ref.shape[0]):
        for j in range(nblk):
            vj = v_ref[j * ATT_BLOCK:(j + 1) * ATT_BLOCK, _head_cols(h)].astype(F32)
            vt_ref[h, :, j * ATT_BLOCK:(j + 1) * ATT_BLOCK] = vj.T.astype(BF16)


def _local_distance():
    qi = lax.broadcasted_iota(jnp.int32, (ATT_BLOCK, ATT_BLOCK), 1)
    ki = lax.broadcasted_iota(jnp.int32, (ATT_BLOCK, ATT_BLOCK), 0)
    return (qi - ki).astype(F32)


def _fold_sublanes(x, op):
    return op(x.reshape(ATT_BLOCK // SUBLANES, SUBLANES, x.shape[-1]), axis=0)


def _softmax_pv(cc, score_fns, s_ref, p_ref, vt_ref, o_ref):
    nk = (cc + 1) * ATT_BLOCK
    blocks = [slice(j * ATT_BLOCK, (j + 1) * ATT_BLOCK) for j in range(cc + 1)]
    maxima = []
    for h, score_fn in enumerate(score_fns):
        m8 = None
        for j, blk in enumerate(blocks):
            s = score_fn(j)
            s_ref[h, blk, :] = s
            part = _fold_sublanes(s, jnp.max)
            m8 = part if m8 is None else jnp.maximum(m8, part)
        maxima.append(jnp.max(m8, axis=0, keepdims=True))
    for h, m in enumerate(maxima):
        l8 = None
        for blk in blocks:
            p = jnp.exp2(s_ref[h, blk, :] - m)
            p_ref[h, blk, :] = p.astype(BF16)
            part = _fold_sublanes(p, jnp.sum)
            l8 = part if l8 is None else l8 + part
        l = jnp.sum(l8, axis=0, keepdims=True)
        acc = _dot(vt_ref[h, :, 0:nk], p_ref[h, 0:nk, :])
        o_ref[:, h * HEAD_DIM:(h + 1) * HEAD_DIM] = (acc / l).T.astype(o_ref.dtype)


def _attn_kernel(slope_ref, q_ref, k_ref, v_ref, mask_ref, o_ref, vt_ref, s_ref, p_ref, bias_ref, kmean_ref, *,
                 nblk, topk):
    c = pl.program_id(1)
    nh = vt_ref.shape[0]
    qk_scale = HEAD_DIM ** -0.5 * LOG2_E

    @pl.when(c == 0)
    def _():
        _stage_values(v_ref, vt_ref, nblk)
        dist = _local_distance()
        for h in range(nh):
            slope2 = slope_ref[h] * LOG2_E
            for diff in range(nblk):
                bias_ref[h, diff] = mask_ref[diff] - slope2 * (dist + float(diff * ATT_BLOCK))
            if topk is not None:
                for j in range(nblk):
                    kj = k_ref[j * ATT_BLOCK:(j + 1) * ATT_BLOCK, _head_cols(h)].astype(F32)
                    kmean_ref[h, j:j + 1, :] = jnp.mean(kj, axis=0, keepdims=True)

    def head_scores(cc, h):
        q = q_ref[:, _head_cols(h)]
        if topk is not None and cc > topk:
            km = kmean_ref[h]
            km_hi = km.astype(BF16)
            km_lo = (km - km_hi.astype(F32)).astype(BF16)
            gate = _dot_nt(km_hi, q) + _dot_nt(km_lo, q)
            g = [gate[n:n + 1, :] for n in range(cc)]
            sel = []
            for n in range(cc):
                n_ahead = jnp.zeros_like(g[n])
                for mth in range(cc):
                    if mth != n:
                        ahead = (g[mth] >= g[n]) if mth < n else (g[mth] > g[n])
                        n_ahead = n_ahead + jnp.where(ahead, 1.0, 0.0)
                sel.append(jnp.where(n_ahead < topk, 0.0, NEG_INF))
        else:
            sel = None

        def score(j):
            kj = k_ref[j * ATT_BLOCK:(j + 1) * ATT_BLOCK, _head_cols(h)]
            s = _dot_nt(kj, q) * qk_scale + bias_ref[h, cc - j]
            return s if (sel is None or j == cc) else s + sel[j]

        return score

    def branch(cc):
        _softmax_pv(cc, [head_scores(cc, h) for h in range(nh)], s_ref, p_ref, vt_ref, o_ref)

    for cc in range(nblk):
        pl.when(c == cc)(functools.partial(branch, cc))


def _block_distances(seq):
    nblk = seq // ATT_BLOCK
    kl = np.arange(ATT_BLOCK)[None, :, None]
    ql = np.arange(ATT_BLOCK)[None, None, :]
    return np.arange(nblk)[:, None, None] * ATT_BLOCK + ql - kl


def _causal_mask(seq):
    return np.where(_block_distances(seq) >= 0, 0.0, NEG_INF).astype(np.float32)


def _dilated_window_log_counts(seq):
    d = _block_distances(seq)
    cnt = np.zeros(d.shape, np.float64)
    for window, dil in DIL_PAIRS:
        cnt += (d >= 0) & (d <= window) & (d % dil == 0)
    return np.where(cnt > 0, np.log2(np.maximum(cnt, 1.0)), NEG_INF).astype(np.float32)


def _attention(z3, slopes, mask, col0, topk, name):
    b, t, _ = z3.shape
    nblk = t // ATT_BLOCK
    nh = slopes.shape[0]
    width = nh * HEAD_DIM
    cb = col0 // width
    return pl.pallas_call(
        functools.partial(_attn_kernel, nblk=nblk, topk=topk),
        grid=(b, nblk),
        in_specs=[
            pl.BlockSpec(memory_space=pltpu.SMEM),
            pl.BlockSpec((None, ATT_BLOCK, width), lambda i, c: (i, c, cb)),
            pl.BlockSpec((None, t, width), lambda i, c: (i, 0, cb + 1)),
            pl.BlockSpec((None, t, width), lambda i, c: (i, 0, cb + 2)),
            pl.BlockSpec((nblk, ATT_BLOCK, ATT_BLOCK), lambda i, c: (0, 0, 0)),
        ],
        out_specs=pl.BlockSpec((None, ATT_BLOCK, width), lambda i, c: (i, c, 0)),
        out_shape=jax.ShapeDtypeStruct((b, t, width), BF16),
        scratch_shapes=[
            pltpu.VMEM((nh, HEAD_DIM, t), BF16),
            pltpu.VMEM((nh, t, ATT_BLOCK), F32),
            pltpu.VMEM((nh, t, ATT_BLOCK), BF16),
            pltpu.VMEM((nh, nblk, ATT_BLOCK, ATT_BLOCK), F32),
            pltpu.VMEM((nh, nblk, HEAD_DIM), F32),
        ],
        compiler_params=_cparams(("parallel", "arbitrary")),
        name=name,
    )(slopes, z3, z3, z3, mask)


def _log_sigmoid(x):
    return jnp.minimum(x, 0.0) - jnp.log(1.0 + jnp.exp(-jnp.abs(x)))


def _split3(x):
    hi = x.astype(BF16)
    r1 = x - hi.astype(F32)
    mid = r1.astype(BF16)
    lo = (r1 - mid.astype(F32)).astype(BF16)
    return hi, mid, lo


def _mlstm_chunk(rows, qk_ref, v_ref, og_ref, gates_t_ref, gbias_ref, gbias_t_ref, cw_ref, cb_ref,
                 ghead_ref, tril_ref, triu_ref, shift_ref, emit, fillers, ext_ref, qk_act_ref, c_ref, n_ref, m_ref):
    L = MLSTM_L
    nqk = H_MLSTM * DK_MLSTM
    fill = iter(fillers)
    next(fill)()

    u_bf = qk_ref[rows, :]
    u = u_bf.astype(F32)
    shifted = _dot(shift_ref[...], u_bf)
    cw = cw_ref[...]
    w_rows = [cw[CONV_WIDTH - 1 - back:CONV_WIDTH - back, :] for back in range(CONV_WIDTH)]
    y = cb_ref[...] + w_rows[0] * u
    ext_ref[SUBLANES:2 * SUBLANES, :] = u[0:SUBLANES, :]
    y_head = cb_ref[...] + w_rows[0] * u[0:SUBLANES, :]
    for back in range(1, CONV_WIDTH):
        y = y + w_rows[back] * shifted[(back - 1) * L:back * L, :]
        y_head = y_head + w_rows[back] * ext_ref[SUBLANES - back:2 * SUBLANES - back, :]
    ext_ref[0:SUBLANES, :] = u[L - SUBLANES:L, :]
    qk_act_ref[...] = y * jax.nn.sigmoid(y)
    qk_act_ref[0:SUBLANES, :] = y_head * jax.nn.sigmoid(y_head)

    next(fill)()
    gates_t = gates_t_ref[:, rows]
    g_row = gates_t[0:N_GATES, :] + gbias_t_ref[...]
    g_col = gates_t.T + gbias_ref[...]
    lf_col = _log_sigmoid(g_col)
    lf_row = _log_sigmoid(g_row)
    b_col_all = sum(_dot(tril_ref[...], t) for t in _split3(lf_col))
    b_row_all = sum(_dot(t, triu_ref[...]) for t in _split3(lf_row))

    ti = lax.broadcasted_iota(jnp.int32, (L, L), 0)
    si = lax.broadcasted_iota(jnp.int32, (L, L), 1)
    causal = si <= ti

    heads = []
    for h in range(H_MLSTM):
        q = qk_act_ref[:, h * DK_MLSTM:(h + 1) * DK_MLSTM]
        k = qk_act_ref[:, nqk + h * DK_MLSTM:nqk + (h + 1) * DK_MLSTM] * (DK_MLSTM ** -0.5)
        qb = q.astype(BF16)
        li_col = g_col[:, h:h + 1]
        li_row = g_row[h:h + 1, :]
        b_col = b_col_all[:, H_MLSTM + h:H_MLSTM + h + 1]
        b_row = b_row_all[H_MLSTM + h:H_MLSTM + h + 1, :]
        m_st = m_ref[h]
        d_log = jnp.where(causal, b_col - b_row + li_row, NEG_INF)
        m_inter = b_col + m_st
        m_t = jnp.maximum(m_inter, jnp.max(d_log, axis=-1, keepdims=True))
        s = _dot_nt(qb, k.astype(BF16)) * jnp.exp(d_log - m_t)
        heads.append(dict(q=q, k=k, qb=qb, s=s, m_t=m_t, w_inter=jnp.exp(m_inter - m_t), m_st=m_st,
                          c_st=c_ref[h], n_st=n_ref[h],
                          li_col=li_col, li_row=li_row, b_col=b_col, b_row=b_row))

    next(fill)()
    for h, hd in enumerate(heads):
        v = v_ref[rows, h * DV_MLSTM:(h + 1) * DV_MLSTM]
        b_last = hd["b_col"][L - 1:L, :]
        g_dec_row = b_last - hd["b_row"] + hd["li_row"]
        g_dec_col = b_last - hd["b_col"] + hd["li_col"]
        m_new = jnp.maximum(b_last + hd["m_st"], jnp.max(g_dec_row, axis=-1, keepdims=True))
        w_old = jnp.exp(b_last + hd["m_st"] - m_new)
        kw = hd["k"] * jnp.exp(g_dec_col - m_new)
        c_ref[h] = w_old * hd["c_st"] + _dot(kw.T.astype(BF16), v)
        n_ref[h] = w_old * hd["n_st"] + jnp.sum(kw, axis=0, keepdims=True)
        m_ref[h] = m_new

    next(fill)()
    for h, hd in enumerate(heads):
        cols = slice(h * DV_MLSTM, (h + 1) * DV_MLSTM)
        v = v_ref[rows, cols]
        num = hd["w_inter"] * _dot(hd["qb"], hd["c_st"].astype(BF16)) + _dot(hd["s"].astype(BF16), v)
        den = (hd["w_inter"] * jnp.sum(hd["q"] * hd["n_st"], axis=-1, keepdims=True)
               + jnp.sum(hd["s"], axis=-1, keepdims=True))
        hh = num / jnp.maximum(jnp.abs(den), jnp.exp(-hd["m_t"]))
        hh = hh * lax.rsqrt(jnp.mean(hh * hh, axis=-1, keepdims=True) + RMS_EPS)
        hh = hh * ghead_ref[:, cols]
        emit(cols, jax.nn.sigmoid(og_ref[rows, cols].astype(F32)) * hh)


def _conv_shift_matrices(length):
    t = np.arange(length)
    mats = [(t[:, None] - t[None, :] == s).astype(np.float32) for s in range(1, CONV_WIDTH)]
    return np.concatenate(mats, axis=0)


def _mixout_kernel(h_ref, a_ref, cd_ref, w_ref, qk_ref, v_ref, og_ref, gates_t_ref, gbias_ref, gbias_t_ref, cw_ref,
                   cb_ref, ghead_ref, tril_ref, triu_ref, shift_ref, o_ref, mo_ref, ext_ref, qk_act_ref,
                   c_ref, n_ref, m_ref, *, n_tiles, tiles_per_seq):
    s = pl.program_id(0)

    @pl.when((s % tiles_per_seq == 0) & (s < n_tiles))
    def _():
        c_ref[...] = jnp.zeros_like(c_ref)
        n_ref[...] = jnp.zeros_like(n_ref)
        m_ref[...] = jnp.zeros_like(m_ref)
        ext_ref[0:SUBLANES, :] = jnp.zeros((SUBLANES, ext_ref.shape[1]), F32)

    na = a_ref.shape[1]
    nm = mo_ref.shape[2]
    n_chunks = TM // MLSTM_L
    n_pieces = 4 * n_chunks
    width = o_ref.shape[1] // n_pieces

    def project(piece):
        cols = slice(piece * width, (piece + 1) * width)
        acc = _dot(a_ref[...], w_ref[0:na, cols])
        acc += _dot(mo_ref[(s + 1) % 2], w_ref[na:na + nm, cols])
        acc += _dot(cd_ref[...], w_ref[na + nm:, cols])
        o_ref[:, cols] = h_ref[:, cols] + acc

    slot = s % 2

    def step(recur, proj):
        if not recur:
            for piece in range(n_pieces):
                project(piece)
            return
        for k in range(n_chunks):
            rows = slice(k * MLSTM_L, (k + 1) * MLSTM_L)

            def emit(cols, value, rows=rows):
                mo_ref[slot, rows, cols] = value.astype(mo_ref.dtype)

            fillers = [functools.partial(project, 4 * k + i) if proj else (lambda: None) for i in range(4)]
            _mlstm_chunk(rows, qk_ref, v_ref, og_ref, gates_t_ref, gbias_ref, gbias_t_ref, cw_ref, cb_ref,
                         ghead_ref, tril_ref, triu_ref, shift_ref, emit, fillers, ext_ref, qk_act_ref,
                         c_ref, n_ref, m_ref)

    pl.when(s == 0)(functools.partial(step, True, False))
    pl.when((s > 0) & (s < n_tiles))(functools.partial(step, True, True))
    pl.when(s == n_tiles)(functools.partial(step, False, True))


def _mixout(h, z, out_a, out_c, w_out, gates_t, gbias, gbias_t, conv_w, conv_b, g_head, tril, triu, shifts,
            layer, seq):
    m = h.shape[0]
    L = MLSTM_L
    nt = m // TM
    nqk = 2 * H_MLSTM * DK_MLSTM
    nv = H_MLSTM * DV_MLSTM
    na, ncol = out_a.shape[1], out_c.shape[1]
    qk_blk = (2 * D_ATT) // nqk
    v_blk = (2 * D_ATT + nqk) // nv
    og_blk = v_blk + 1
    prev = lambda s: (jnp.maximum(s - 1, 0), 0)
    cur = lambda s: jnp.minimum(s, nt - 1)
    const = lambda s: (0, 0)
    return pl.pallas_call(
        functools.partial(_mixout_kernel, n_tiles=nt, tiles_per_seq=seq // TM),
        grid=(nt + 1,),
        in_specs=[
            pl.BlockSpec((TM, D_MODEL), prev),
            pl.BlockSpec((TM, na), prev),
            pl.BlockSpec((TM, ncol), prev),
            pl.BlockSpec((na + nv + ncol, D_MODEL), const),
            pl.BlockSpec((TM, nqk), lambda s: (cur(s), qk_blk)),
            pl.BlockSpec((TM, nv), lambda s: (cur(s), v_blk)),
            pl.BlockSpec((TM, nv), lambda s: (cur(s), og_blk)),
            pl.BlockSpec((LANES, TM), lambda s: (0, cur(s))),
            pl.BlockSpec((None, 1, LANES), lambda s: (layer, 0, 0)),
            pl.BlockSpec((None, N_GATES, 1), lambda s: (layer, 0, 0)),
            pl.BlockSpec((None, CONV_WIDTH, nqk), lambda s: (layer, 0, 0)),
            pl.BlockSpec((None, 1, nqk), lambda s: (layer, 0, 0)),
            pl.BlockSpec((None, 1, nv), lambda s: (layer, 0, 0)),
            pl.BlockSpec((L, L), const),
            pl.BlockSpec((L, L), const),
            pl.BlockSpec(((CONV_WIDTH - 1) * L, L), const),
        ],
        out_specs=pl.BlockSpec((TM, D_MODEL), prev),
        out_shape=jax.ShapeDtypeStruct((m, D_MODEL), F32),
        scratch_shapes=[
            pltpu.VMEM((2, TM, nv), BF16),
            pltpu.VMEM((2 * SUBLANES, nqk), F32),
            pltpu.VMEM((L, nqk), F32),
            pltpu.VMEM((H_MLSTM, DK_MLSTM, DV_MLSTM), F32),
            pltpu.VMEM((H_MLSTM, 1, DK_MLSTM), F32),
            pltpu.VMEM((H_MLSTM, 1, 1), F32),
        ],
        compiler_params=_cparams(("arbitrary",)),
        name="mixout",
    )(h, out_a, out_c, w_out, z, z, z, gates_t, gbias, gbias_t, conv_w, conv_b, g_head, tril, triu, shifts)


def _ple_kernel(h_ref, g_ref, p_ref, wg_ref, wp_ref, gfin_ref, *refs, final):
    n_cast = len(refs) // 2
    o_ref = refs[n_cast]
    x = h_ref[...]
    xn = _rms(x, g_ref[...]).astype(BF16)
    gate = jax.nn.sigmoid(_dot(xn, wg_ref[...]))
    proj = _dot(p_ref[...].astype(BF16), wp_ref[...])
    y = x + gate * proj
    if final:
        y = _rms(y, gfin_ref[...])
    o_ref[...] = y
    _cast_blocks(refs[:n_cast], refs[n_cast + 1:])


def _ple(h, g, p2, w_gate, w_proj, g_final, layer, next_w_in_t=None):
    m = h.shape[0]
    nm = m // TM
    in_specs = [
        pl.BlockSpec((TM, D_MODEL), lambda i: (i, 0)),
        pl.BlockSpec((None, 1, D_MODEL), lambda i: (layer, 0, 0)),
        pl.BlockSpec((None, TM, D_PLE), lambda i: (layer, i, 0)),
        pl.BlockSpec((D_MODEL, D_MODEL), lambda i: (0, 0)),
        pl.BlockSpec((D_PLE, D_MODEL), lambda i: (0, 0)),
        pl.BlockSpec((1, D_MODEL), lambda i: (0, 0)),
    ]
    out_specs = [pl.BlockSpec((TM, D_MODEL), lambda i: (i, 0))]
    out_shape = [jax.ShapeDtypeStruct((m, D_MODEL), F32)]
    inputs = [h, g, p2, w_gate, w_proj, g_final]
    if next_w_in_t is not None:
        src, nl = next_w_in_t
        blk = (D_Z // nm, D_MODEL)
        in_specs.append(pl.BlockSpec((None, *blk), lambda i: (nl, i, 0)))
        out_specs.append(pl.BlockSpec(blk, lambda i: (i, 0)))
        out_shape.append(jax.ShapeDtypeStruct((D_Z, D_MODEL), BF16))
        inputs.append(src)
    outs = pl.pallas_call(
        functools.partial(_ple_kernel, final=next_w_in_t is None),
        grid=(nm,),
        in_specs=in_specs,
        out_specs=out_specs,
        out_shape=out_shape,
        compiler_params=_cparams(("parallel",)),
        name="ple",
    )(*inputs)
    return outs[0], (outs[1] if len(outs) > 1 else None)


def _cast_kernel(src_ref, dst_ref):
    dst_ref[...] = src_ref[...].astype(BF16)


def _cast_rows(src, layer, rows):
    cols = src.shape[2]
    return pl.pallas_call(
        _cast_kernel,
        grid=(rows // TM,),
        in_specs=[pl.BlockSpec((None, TM, cols), lambda i: (layer, i, 0))],
        out_specs=pl.BlockSpec((TM, cols), lambda i: (i, 0)),
        out_shape=jax.ShapeDtypeStruct((rows, cols), BF16),
        compiler_params=_cparams(("parallel",)),
        name="cast_rows",
    )(src)


def kernel(x, p, g_ffn1, w_up1, w_down1, g_mix, w_in, conv_w, conv_b, b_igate, b_fgate, g_head, w_out, g_ffn2, w_up2, w_down2, g_ple, w_ple_gate, w_ple_proj, g_final):
    b, t, d = x.shape
    depth = p.shape[0]
    m = b * t

    ffn_w = (w_up1[0].astype(BF16), w_down1[0].astype(BF16))
    w_in_t = jnp.swapaxes(w_in, 1, 2)
    w_in_tb = _cast_rows(w_in_t, 0, D_Z)
    w_gate_t = jnp.pad(w_in_t[:, D_Z:, :], ((0, 0), (0, LANES - N_GATES), (0, 0)))
    gate_bias = jnp.concatenate([b_igate, b_fgate], axis=-1)
    gbias_col = jnp.pad(gate_bias, ((0, 0), (0, LANES - N_GATES)))[:, None, :]
    gbias_row = gate_bias[:, :, None]
    row = lambda a: a[:, None, :]
    g_ffn1r, g_mixr, g_ffn2r, g_pler = row(g_ffn1), row(g_mix), row(g_ffn2), row(g_ple)
    conv_br, g_headr = row(conv_b), row(g_head)
    g_finalr = g_final[None, :]
    p2 = p.reshape(depth, m, D_PLE)

    slopes = 2.0 ** (-8.0 * jnp.arange(1, N_ALIBI + 1, dtype=F32) / N_ALIBI)
    slopes_moba, slopes_dil = slopes[0::2], slopes[1::2]
    causal_mask = jnp.asarray(_causal_mask(t))
    log_counts = jnp.asarray(_dilated_window_log_counts(t))
    tri = np.tril(np.ones((MLSTM_L, MLSTM_L), np.float32))
    tril, triu = jnp.asarray(tri, BF16), jnp.asarray(tri.T, BF16)
    shifts = jnp.asarray(_conv_shift_matrices(MLSTM_L), BF16)

    h = x.reshape(m, d)
    for i in range(depth):
        h, ffn_w = _ffn(h, g_ffn1r, *ffn_w, i, next_weights=(w_up2, w_down2, i))
        z, gates_t, (w_outb, w_pgb, w_ppb) = _inproj(
            h, g_mixr, w_in_tb, w_gate_t, i, (w_out, w_ple_gate, w_ple_proj))
        z3 = z.reshape(b, t, D_Z)
        out_a = _attention(z3, slopes_moba, causal_mask, 0, MOBA_TOPK, "moba")
        out_c = _attention(z3, slopes_dil, log_counts, D_ATT, None, "dilated")
        h = _mixout(h, z, out_a.reshape(m, -1), out_c.reshape(m, -1), w_outb, gates_t, gbias_col, gbias_row,
                    conv_w, conv_br, g_headr, tril, triu, shifts, i, t)
        upcoming = (w_up1, w_down1, i + 1) if i + 1 < depth else None
        h, ffn_w = _ffn(h, g_ffn2r, *ffn_w, i, next_weights=upcoming)
        h, w_in_tb = _ple(h, g_pler, p2, w_pgb, w_ppb, g_finalr, i,
                          next_w_in_t=(w_in_t, i + 1) if i + 1 < depth else None)
    return h.reshape(b, t, d)
```
